```python
import jax, jax.numpy as jnp
from jax import lax
import numpy as np

D_MODEL = 1024
BATCH = 8
SEQ = 4096
DEPTH = 2

N_EVEN = (DEPTH + 1) // 2
N_ODD = DEPTH // 2
RMS_EPS = 1e-5
LN_EPS = 1e-5
D_FF = 4 * D_MODEL

GM_WIDTH = D_MODEL
GM_GROUPS = 8
GM_CH = GM_WIDTH // GM_GROUPS
GM_CHUNK = 128

SSM_D_INNER = D_MODEL
SSM_HEADDIM = 64
SSM_HEADS = SSM_D_INNER // SSM_HEADDIM
SSM_GROUPS = 4
SSM_STATE = 128
SSM_CONV = 4
SSD_CHUNK = 128
SSM_CONV_DIM = SSM_D_INNER + 2 * SSM_GROUPS * SSM_STATE

IN_EVEN = 2 * GM_WIDTH + SSM_D_INNER + SSM_CONV_DIM + SSM_HEADS
MIX_EVEN = GM_WIDTH + SSM_D_INNER

ATTN_HEADS = 16
ATTN_KV_HEADS = 2
ATTN_HEAD_DIM = 64
ATTN_WINDOW = 128
ATTN_BLOCK = ATTN_WINDOW
QKV_DIM = (ATTN_HEADS + 2 * ATTN_KV_HEADS) * ATTN_HEAD_DIM

kernel_name = "hybrid_gmlp_ssd_swa_sinks"


def rmsnorm(x, g, eps=RMS_EPS):
    xf = x.astype(jnp.float32)
    y = xf * lax.rsqrt(jnp.mean(xf * xf, axis=-1, keepdims=True) + eps)
    return (y * g.astype(jnp.float32)).astype(x.dtype)


def layernorm(x, g, b, eps=LN_EPS):
    xf = x.astype(jnp.float32)
    mu = jnp.mean(xf, axis=-1, keepdims=True)
    var = jnp.mean(jnp.square(xf - mu), axis=-1, keepdims=True)
    y = (xf - mu) * lax.rsqrt(var + eps)
    return (y * g.astype(jnp.float32) + b.astype(jnp.float32)).astype(x.dtype)


def gmlp_spatial_gating(u, v, ln_g, ln_b, w_s, b_s):
    b, s, _ = v.shape
    nc = s // GM_CHUNK
    vn = layernorm(v, ln_g, ln_b).reshape(b, nc, GM_CHUNK, GM_GROUPS, GM_CH)
    causal = jnp.tril(jnp.ones((GM_CHUNK, GM_CHUNK), dtype=bool))
    w = jnp.where(causal, w_s, 0.0).astype(v.dtype)
    mixed = jnp.einsum('gts,bcsgd->bctgd', w, vn) + b_s.T[None, None, :, :, None]
    return u * mixed.reshape(b, s, GM_WIDTH)


def causal_depthwise_conv(x, w, bias):
    k = w.shape[0]
    s = x.shape[1]
    xp = jnp.pad(x, ((0, 0), (k - 1, 0), (0, 0)))
    return sum(xp[:, i:i + s] * w[i] for i in range(k)) + bias


def ssd_chunked(x, dt, a_neg, bm, cm):
    b, s, h, p = x.shape
    g, n = bm.shape[2], bm.shape[3]
    r = h // g
    l = SSD_CHUNK
    c = s // l
    xc = x.reshape(b, c, l, g, r, p)
    bc = bm.reshape(b, c, l, g, n)
    cc = cm.reshape(b, c, l, g, n)
    dtc = dt.reshape(b, c, l, g, r)
    a_cum = jnp.cumsum(dtc * a_neg.reshape(g, r), axis=2)
    causal = jnp.tril(jnp.ones((l, l), dtype=bool))[:, :, None, None]
    seg = a_cum[:, :, :, None] - a_cum[:, :, None, :]
    decay = jnp.exp(jnp.where(causal, seg, -jnp.inf))
    cb = jnp.einsum('bclgn,bcsgn->bclsg', cc, bc)
    w_intra = (cb[..., None] * decay * dtc[:, :, None]).astype(x.dtype)
    y_diag = jnp.einsum('bclsgr,bcsgrp->bclgrp', w_intra, xc)
    to_end = (jnp.exp(a_cum[:, :, -1:] - a_cum) * dtc).astype(x.dtype)
    states = jnp.einsum('bclgn,bclgr,bclgrp->bcgrpn', bc, to_end, xc)
    chunk_decay = jnp.exp(a_cum[:, :, -1]).astype(x.dtype)

    def step(hstate, inp):
        dec, st = inp
        return dec[..., None, None] * hstate + st, hstate

    h0 = jnp.zeros((b, g, r, p, n), x.dtype)
    _, prev = lax.scan(step, h0, (jnp.moveaxis(chunk_decay, 1, 0), jnp.moveaxis(states, 1, 0)))
    prev = jnp.moveaxis(prev, 0, 1)
    y_off = jnp.einsum('bclgn,bcgrpn,bclgr->bclgrp', cc, prev, jnp.exp(a_cum).astype(x.dtype))
    return (y_diag + y_off).reshape(b, s, h, p)


def ssd_mixer(z, xbc, dt_raw, conv_w, conv_b, dt_bias, a_log, d_skip, norm_g):
    b, s, _ = z.shape
    xbc = jax.nn.silu(causal_depthwise_conv(xbc, conv_w, conv_b))
    xs, bm, cm = jnp.split(xbc, [SSM_D_INNER, SSM_D_INNER + SSM_GROUPS * SSM_STATE], axis=-1)
    xs = xs.reshape(b, s, SSM_HEADS, SSM_HEADDIM)
    bm = bm.reshape(b, s, SSM_GROUPS, SSM_STATE)
    cm = cm.reshape(b, s, SSM_GROUPS, SSM_STATE)
    dt = jax.nn.softplus(dt_raw.astype(jnp.float32) + dt_bias.astype(jnp.float32))
    a_neg = -jnp.exp(a_log.astype(jnp.float32))
    y = ssd_chunked(xs, dt, a_neg, bm, cm) + xs * d_skip[:, None]
    y = y.reshape(b, s, SSM_D_INNER) * jax.nn.silu(z)
    y = rmsnorm(y.reshape(b, s, SSM_GROUPS, -1), norm_g.reshape(SSM_GROUPS, -1))
    return y.reshape(b, s, SSM_D_INNER)


def even_mixer(h, w_in, w_out, gm_ln_g, gm_ln_b, gm_w_s, gm_b_s,
               conv_w, conv_b, dt_bias, a_log, d_skip, ssm_norm_g):
    proj = h @ w_in
    u, v, z, xbc, dt_raw = jnp.split(
        proj, [GM_WIDTH, 2 * GM_WIDTH, 2 * GM_WIDTH + SSM_D_INNER,
               2 * GM_WIDTH + SSM_D_INNER + SSM_CONV_DIM], axis=-1)
    a_out = gmlp_spatial_gating(jax.nn.gelu(u), jax.nn.gelu(v), gm_ln_g, gm_ln_b, gm_w_s, gm_b_s)
    b_out = ssd_mixer(z, xbc, dt_raw, conv_w, conv_b, dt_bias, a_log, d_skip, ssm_norm_g)
    return jnp.concatenate([a_out, b_out], axis=-1) @ w_out


def sliding_window_sink_attention(q, k, v, sinks):
    b, s, _, d = q.shape
    blk = ATTN_BLOCK
    nb = s // blk
    grp = ATTN_HEADS // ATTN_KV_HEADS
    qb = q.reshape(b, nb, blk, ATTN_KV_HEADS, grp, d)
    kp = jnp.pad(k, ((0, 0), (blk, 0), (0, 0), (0, 0))).reshape(b, nb + 1, blk, ATTN_KV_HEADS, d)
    vp = jnp.pad(v, ((0, 0), (blk, 0), (0, 0), (0, 0))).reshape(b, nb + 1, blk, ATTN_KV_HEADS, d)
    kband = jnp.concatenate([kp[:, :-1], kp[:, 1:]], axis=2)
    vband = jnp.concatenate([vp[:, :-1], vp[:, 1:]], axis=2)
    scores = jnp.einsum('bnqkgd,bnskd->bnkgqs', qb, kband).astype(jnp.float32) * (d ** -0.5)
    qpos = jnp.arange(nb)[:, None, None] * blk + jnp.arange(blk)[None, :, None]
    kpos = jnp.arange(nb)[:, None, None] * blk - blk + jnp.arange(2 * blk)[None, None, :]
    rel = qpos - kpos
    valid = (rel >= 0) & (rel < ATTN_WINDOW) & (kpos >= 0)
    scores = jnp.where(valid[None, :, None, None], scores, -jnp.inf)
    sink = sinks.astype(jnp.float32).reshape(ATTN_KV_HEADS, grp)[None, None, :, :, None, None]
    m = jnp.maximum(jnp.max(scores, axis=-1, keepdims=True), sink)
    pexp = jnp.exp(scores - m)
    denom = jnp.sum(pexp, axis=-1, keepdims=True) + jnp.exp(sink - m)
    probs = (pexp / denom).astype(v.dtype)
    out = jnp.einsum('bnkgqs,bnskd->bnqkgd', probs, vband)
    return out.reshape(b, s, ATTN_HEADS * d)


def odd_mixer(h, w_qkv, b_qkv, w_o, b_o, sinks):
    b, s, _ = h.shape
    qkv = h @ w_qkv + b_qkv
    q, k, v = jnp.split(qkv, [ATTN_HEADS * ATTN_HEAD_DIM, (ATTN_HEADS + ATTN_KV_HEADS) * ATTN_HEAD_DIM], axis=-1)
    q = q.reshape(b, s, ATTN_HEADS, ATTN_HEAD_DIM)
    k = k.reshape(b, s, ATTN_KV_HEADS, ATTN_HEAD_DIM)
    v = v.reshape(b, s, ATTN_KV_HEADS, ATTN_HEAD_DIM)
    return sliding_window_sink_attention(q, k, v, sinks) @ w_o + b_o


def squared_relu_mlp(h, w_up, w_down):
    return jnp.square(jax.nn.relu(h @ w_up)) @ w_down


def setup_inputs(seed: int = 0) -> dict:
    key = jax.random.key(seed)
    ks = jax.random.split(key, 24)
    nrm = jax.random.normal
    f32 = jnp.float32
    dt0 = jnp.exp(jax.random.uniform(ks[10], (N_EVEN, SSM_HEADS), f32, np.log(1e-3), np.log(1e-1)))
    return {
        "x": nrm(ks[0], (BATCH, SEQ, D_MODEL), f32),
        "norm_mix_g": 1.0 + 0.02 * nrm(ks[1], (DEPTH, D_MODEL), f32),
        "norm_mlp_g": 1.0 + 0.02 * nrm(ks[2], (DEPTH, D_MODEL), f32),
        "final_norm_g": 1.0 + 0.02 * nrm(ks[3], (D_MODEL,), f32),
        "w_in_even": nrm(ks[4], (N_EVEN, D_MODEL, IN_EVEN), f32) * D_MODEL ** -0.5,
        "w_out_even": nrm(ks[5], (N_EVEN, MIX_EVEN, D_MODEL), f32) * MIX_EVEN ** -0.5,
        "gm_ln_g": 1.0 + 0.02 * nrm(ks[6], (N_EVEN, GM_WIDTH), f32),
        "gm_ln_b": 0.02 * nrm(ks[7], (N_EVEN, GM_WIDTH), f32),
        "gm_w_s": nrm(ks[8], (N_EVEN, GM_GROUPS, GM_CHUNK, GM_CHUNK), f32) * 0.5 * GM_CHUNK ** -0.5,
        "gm_b_s": 1.0 + 0.02 * nrm(ks[9], (N_EVEN, GM_GROUPS, GM_CHUNK), f32),
        "ssm_conv_w": nrm(ks[11], (N_EVEN, SSM_CONV, SSM_CONV_DIM), f32) * SSM_CONV ** -0.5,
        "ssm_conv_b": 0.02 * nrm(ks[12], (N_EVEN, SSM_CONV_DIM), f32),
        "ssm_dt_bias": dt0 + jnp.log(-jnp.expm1(-dt0)),
        "ssm_a_log": jnp.log(jax.random.uniform(ks[13], (N_EVEN, SSM_HEADS), f32, 1.0, 16.0)),
        "ssm_d": 1.0 + 0.1 * nrm(ks[14], (N_EVEN, SSM_HEADS), f32),
        "ssm_norm_g": 1.0 + 0.02 * nrm(ks[15], (N_EVEN, SSM_D_INNER), f32),
        "w_qkv": nrm(ks[16], (N_ODD, D_MODEL, QKV_DIM), f32) * D_MODEL ** -0.5,
        "b_qkv": 0.02 * nrm(ks[17], (N_ODD, QKV_DIM), f32),
        "w_o": nrm(ks[18], (N_ODD, ATTN_HEADS * ATTN_HEAD_DIM, D_MODEL), f32) * (ATTN_HEADS * ATTN_HEAD_DIM) ** -0.5,
        "b_o": 0.02 * nrm(ks[19], (N_ODD, D_MODEL), f32),
        "attn_sinks": 0.5 * nrm(ks[20], (N_ODD, ATTN_HEADS), f32),
        "w_up": nrm(ks[21], (DEPTH, D_MODEL, D_FF), f32) * D_MODEL ** -0.5,
        "w_down": nrm(ks[22], (DEPTH, D_FF, D_MODEL), f32) * D_FF ** -0.5,
    }


def reference(x, norm_mix_g, norm_mlp_g, final_norm_g, w_in_even, w_out_even,
              gm_ln_g, gm_ln_b, gm_w_s, gm_b_s, ssm_conv_w, ssm_conv_b,
              ssm_dt_bias, ssm_a_log, ssm_d, ssm_norm_g,
              w_qkv, b_qkv, w_o, b_o, attn_sinks, w_up, w_down):
    h = x
    for i in range(DEPTH):
        j = i // 2
        y = rmsnorm(h, norm_mix_g[i])
        if i % 2 == 0:
            h = h + even_mixer(y, w_in_even[j], w_out_even[j], gm_ln_g[j], gm_ln_b[j],
                               gm_w_s[j], gm_b_s[j], ssm_conv_w[j], ssm_conv_b[j],
                               ssm_dt_bias[j], ssm_a_log[j], ssm_d[j], ssm_norm_g[j])
        else:
            h = h + odd_mixer(y, w_qkv[j], b_qkv[j], w_o[j], b_o[j], attn_sinks[j])
        y = rmsnorm(h, norm_mlp_g[i])
        h = h + squared_relu_mlp(y, w_up[i], w_down[i])
    return rmsnorm(h, final_norm_g)
```

```python
import functools

import jax
import jax.numpy as jnp
from jax import lax
from jax.experimental import pallas as pl
from jax.experimental.pallas import tpu as pltpu

F32 = jnp.float32
BF16 = jnp.bfloat16

RMS_EPS = 1e-5
LN_EPS = 1e-5

LANES = 128
CHUNK = 128
HEADDIM = 64
HALF = LANES // 2

GM_GROUPS = 8
SSM_HEADS = 16
SSM_GROUPS = 4
SSM_STATE = 128
SSM_CONV = 4
CONV_CARRY = 8

ATTN_HEADS = 16
ATTN_KV_HEADS = 2
ATTN_PAIRS = ATTN_HEADS // 2
PAIRS_PER_KV = ATTN_PAIRS // ATTN_KV_HEADS

VMEM_LIMIT = 56 * 1024 * 1024


def _dot(a, b):
    return jnp.dot(a, b, preferred_element_type=F32)


def _dot_nt(a, b):
    return lax.dot_general(a, b, (((1,), (1,)), ((), ())), preferred_element_type=F32)


def _dot_exact(a, b):
    return jnp.dot(a, b, preferred_element_type=F32, precision=lax.Precision.HIGHEST)


def _rmsnorm(x, g):
    ms = jnp.mean(x * x, axis=-1, keepdims=True)
    return x * lax.rsqrt(ms + RMS_EPS) * g


def _gelu_tanh(x):
    c = 0.7978845608028654
    return 0.5 * x * (1.0 + jnp.tanh(c * (x + 0.044715 * (x * x * x))))


def _sigmoid(x):
    return 1.0 / (1.0 + jnp.exp(-x))


def _softplus(x):
    return jnp.maximum(x, 0.0) + jnp.log1p(jnp.exp(-jnp.abs(x)))


def _mixer0_kernel(x_ref, g_ref, w_in_ref, w_out_ref, lng_ref, lnb_ref, ws_ref, bexp_ref,
                   convw_ref, convb_ref, dtb_ref, alog_ref, dexp_ref, ng_ref, hexp_ref,
                   out_ref,
                   xpad_s, xs_s, xlo_s, xhi_s, b_s, c_s, dt_s, a_s, y_s, mix_s, state_s,
                   *, tile, d_model, d_inner):
    n_chunks = tile // CHUNK
    d_bc = SSM_GROUPS * SSM_STATE

    @pl.when(pl.program_id(1) == 0)
    def _():
        state_s[...] = jnp.zeros_like(state_s)
        xpad_s[0:CONV_CARRY, :] = jnp.zeros((CONV_CARRY, xpad_s.shape[1]), F32)

    x = x_ref[0]
    y = _rmsnorm(x, g_ref[...]).astype(BF16)

    row = lax.broadcasted_iota(jnp.int32, (CHUNK, CHUNK), 0)
    col = lax.broadcasted_iota(jnp.int32, (CHUNK, CHUNK), 1)
    causal = row >= col

    o_u, o_v, o_z, o_x = 0, d_model, 2 * d_model, 2 * d_model + d_inner
    o_dt = o_x + d_inner + 2 * d_bc
    u = _gelu_tanh(_dot(y, w_in_ref[:, o_u:o_u + d_model]))
    v = _gelu_tanh(_dot(y, w_in_ref[:, o_v:o_v + d_model]))
    mu = jnp.mean(v, axis=-1, keepdims=True)
    vc = v - mu
    var = jnp.mean(vc * vc, axis=-1, keepdims=True)
    vn = (vc * lax.rsqrt(var + LN_EPS) * lng_ref[...] + lnb_ref[...]).astype(BF16)
    for g in range(GM_GROUPS):
        gs = slice(g * CHUNK, (g + 1) * CHUNK)
        w_g = jnp.where(causal, ws_ref[g], 0.0).astype(BF16)
        for c in range(n_chunks):
            cs = slice(c * CHUNK, (c + 1) * CHUNK)
            mixed = _dot(w_g, vn[cs, gs]) + bexp_ref[:, gs]
            mix_s[cs, gs] = (u[cs, gs] * mixed).astype(BF16)

    z = _dot(y, w_in_ref[:, o_z:o_z + d_inner])
    xpad_s[CONV_CARRY:CONV_CARRY + tile, :] = _dot(y, w_in_ref[:, o_x:o_dt])
    conv = convb_ref[...]
    for k in range(SSM_CONV):
        off = CONV_CARRY - (SSM_CONV - 1) + k
        conv = conv + convw_ref[k:k + 1, :] * xpad_s[off:off + tile, :]
    xpad_s[0:CONV_CARRY, :] = xpad_s[tile:tile + CONV_CARRY, :]
    xbc = conv * _sigmoid(conv)
    xs = xbc[:, 0:d_inner]
    xs_s[...] = xs
    lo_full = lax.broadcasted_iota(jnp.int32, (1, d_inner), 1) % LANES < HALF
    xlo_s[...] = jnp.where(lo_full, xs, 0.0).astype(BF16)
    xhi_s[...] = jnp.where(lo_full, 0.0, xs).astype(BF16)
    b_s[...] = xbc[:, d_inner:d_inner + d_bc]
    c_s[...] = xbc[:, d_inner + d_bc:d_inner + 2 * d_bc]
    dt = _softplus(_dot(y, w_in_ref[:, o_dt:o_dt + LANES]) + dtb_ref[...])
    dt_s[...] = dt
    a_s[...] = dt * (-jnp.exp(alog_ref[...]))

    tril = jnp.where(causal, 1.0, 0.0).astype(F32)

    def chunk_body(c, carry):
        r = pl.ds(pl.multiple_of(c * CHUNK, CHUNK), CHUNK)
        dtc = dt_s[r, :]
        acum = _dot_exact(tril, a_s[r, :])
        acum_t = acum.T
        dt_t = dtc.T
        last = acum[CHUNK - 1:CHUNK, :]
        te_t = (jnp.exp(last - acum) * dtc).T
        decay8 = _dot_exact(jnp.exp(acum[CHUNK - 8:CHUNK, :]), hexp_ref[...])
        chunk_decay = decay8[7:8, :]

        state = state_s[...]
        st_lo = jnp.where(lo_full, state, 0.0).astype(BF16)
        st_hi = jnp.where(lo_full, 0.0, state).astype(BF16)

        y_pairs = []
        s_pairs = []
        pairs_per_group = SSM_HEADS // SSM_GROUPS // 2
        for j in range(SSM_HEADS // 2):
            ps = slice(j * LANES, (j + 1) * LANES)
            if j % pairs_per_group == 0:
                g = j // pairs_per_group
                ns = slice(g * SSM_STATE, (g + 1) * SSM_STATE)
                bg = b_s[r, ns]
                cg = c_s[r, ns]
                cb = _dot_nt(cg.astype(BF16), bg.astype(BF16))
                bg_t = bg.T
            y_acc = None
            s_acc = None
            for h, x_sel, st_sel in ((2 * j, xlo_s, st_lo), (2 * j + 1, xhi_s, st_hi)):
                a_col = jnp.broadcast_to(acum[:, h:h + 1], (CHUNK, CHUNK))
                seg = a_col - acum_t[h:h + 1, :]
                decay = jnp.exp(jnp.where(causal, seg, -jnp.inf))
                w_intra = (cb * decay * dt_t[h:h + 1, :]).astype(BF16)
                c_in = (cg * jnp.exp(a_col)).astype(BF16)
                xh = x_sel[r, ps]
                lhs = jnp.concatenate([w_intra, c_in], axis=1)
                rhs = jnp.concatenate([xh, st_sel[:, ps]], axis=0)
                yh = _dot(lhs, rhs)
                sh = _dot((bg_t * te_t[h:h + 1, :]).astype(BF16), xh)
                y_acc = yh if y_acc is None else y_acc + yh
                s_acc = sh if s_acc is None else s_acc + sh
            y_pairs.append(y_acc)
            s_pairs.append(s_acc)
        y_s[r, :] = jnp.concatenate(y_pairs, axis=1)
        state_s[...] = state * chunk_decay + jnp.concatenate(s_pairs, axis=1)
        return carry

    lax.fori_loop(0, n_chunks, chunk_body, 0)

    yv = y_s[...] + xs_s[...] * dexp_ref[...]
    yv = yv * (z * _sigmoid(z))
    gw = d_inner // SSM_GROUPS
    for g in range(SSM_GROUPS):
        gs = slice(g * gw, (g + 1) * gw)
        mix_s[:, d_model + g * gw:d_model + (g + 1) * gw] = _rmsnorm(
            yv[:, gs], ng_ref[:, gs]).astype(BF16)

    out_ref[0] = x + _dot(mix_s[...], w_out_ref[...])


def _mixer0(h, g, w_in, w_out, ln_g, ln_b, w_s, b_s, conv_w, conv_b, dt_bias, a_log, d_skip,
            norm_g, *, tile=256):
    bsz, seq, d_model = h.shape
    d_inner = d_model
    d_bc = SSM_GROUPS * SSM_STATE
    conv_dim = d_inner + 2 * d_bc
    n_main = 3 * d_model + conv_dim
    assert w_in.shape == (d_model, n_main + SSM_HEADS)
    assert seq % tile == 0 and tile % CHUNK == 0 and d_inner == SSM_HEADS * HEADDIM

    pad = LANES - SSM_HEADS
    w_in_p = jnp.pad(w_in, ((0, 0), (0, pad))).astype(BF16)
    row = lambda v: v.reshape(1, -1).astype(F32)
    pad_row = lambda v: jnp.pad(v.astype(F32), (0, pad)).reshape(1, LANES)
    bexp = jnp.repeat(b_s.T.astype(F32), CHUNK, axis=1)
    dexp = jnp.repeat(d_skip.astype(F32), HEADDIM).reshape(1, d_inner)
    hexp = (jnp.arange(LANES)[:, None] == (jnp.arange(d_inner)[None, :] // HEADDIM)).astype(F32)

    const = lambda shape: pl.BlockSpec(shape, lambda b, t: (0,) * len(shape))
    kern = functools.partial(_mixer0_kernel, tile=tile, d_model=d_model, d_inner=d_inner)
    return pl.pallas_call(
        kern,
        grid=(bsz, seq // tile),
        in_specs=[
            pl.BlockSpec((1, tile, d_model), lambda b, t: (b, t, 0)),
            const((1, d_model)),
            const(w_in_p.shape),
            const(w_out.shape),
            const((1, d_model)), const((1, d_model)),
            const(w_s.shape),
            const(bexp.shape),
            const(conv_w.shape), const((1, conv_dim)),
            const((1, LANES)), const((1, LANES)),
            const((1, d_inner)), const((1, d_inner)),
            const(hexp.shape),
        ],
        out_specs=pl.BlockSpec((1, tile, d_model), lambda b, t: (b, t, 0)),
        out_shape=jax.ShapeDtypeStruct(h.shape, F32),
        scratch_shapes=[
            pltpu.VMEM((tile + CONV_CARRY, conv_dim), F32),
            pltpu.VMEM((tile, d_inner), F32),
            pltpu.VMEM((tile, d_inner), BF16),
            pltpu.VMEM((tile, d_inner), BF16),
            pltpu.VMEM((tile, d_bc), F32),
            pltpu.VMEM((tile, d_bc), F32),
            pltpu.VMEM((tile, LANES), F32),
            pltpu.VMEM((tile, LANES), F32),
            pltpu.VMEM((tile, d_inner), F32),
            pltpu.VMEM((tile, d_model + d_inner), BF16),
            pltpu.VMEM((SSM_STATE, d_inner), F32),
        ],
        compiler_params=pltpu.CompilerParams(
            dimension_semantics=("arbitrary", "arbitrary"),
            vmem_limit_bytes=VMEM_LIMIT),
        name="mixer0",
    )(h, row(g), w_in_p, w_out.astype(BF16), row(ln_g), row(ln_b), w_s.astype(F32), bexp,
      conv_w.astype(F32), row(conv_b), pad_row(dt_bias), pad_row(a_log), dexp, row(norm_g),
      hexp)


def _mlp_kernel(x_ref, g_ref, w_up_ref, w_down_ref, gf_ref, out_ref, *, ff_chunk, final_norm):
    x = x_ref[...]
    y = _rmsnorm(x, g_ref[...]).astype(BF16)
    acc = x
    d_ff = w_up_ref.shape[1]
    for c in range(d_ff // ff_chunk):
        cs = slice(c * ff_chunk, (c + 1) * ff_chunk)
        hid = jnp.maximum(_dot(y, w_up_ref[:, cs]), 0.0)
        acc = acc + _dot((hid * hid).astype(BF16), w_down_ref[cs, :])
    if final_norm:
        acc = _rmsnorm(acc, gf_ref[...])
    out_ref[...] = acc


def _mlp(h, g, w_up, w_down, g_final, *, final_norm, tile=512, ff_chunk=1024):
    bsz, seq, d_model = h.shape
    n = bsz * seq
    d_ff = w_up.shape[1]
    assert n % tile == 0 and d_ff % ff_chunk == 0
    x2 = h.reshape(n, d_model)
    const = lambda shape: pl.BlockSpec(shape, lambda i: (0,) * len(shape))
    out = pl.pallas_call(
        functools.partial(_mlp_kernel, ff_chunk=ff_chunk, final_norm=final_norm),
        grid=(n // tile,),
        in_specs=[
            pl.BlockSpec((tile, d_model), lambda i: (i, 0)),
            const((1, d_model)),
            const((d_model, d_ff)),
            const((d_ff, d_model)),
            const((1, d_model)),
        ],
        out_specs=pl.BlockSpec((tile, d_model), lambda i: (i, 0)),
        out_shape=jax.ShapeDtypeStruct((n, d_model), F32),
        compiler_params=pltpu.CompilerParams(
            dimension_semantics=("arbitrary",), vmem_limit_bytes=VMEM_LIMIT),
        name="mlp_final" if final_norm else "mlp",
    )(x2, g.reshape(1, -1).astype(F32), w_up.astype(BF16), w_down.astype(BF16),
      g_final.reshape(1, -1).astype(F32))
    return out.reshape(bsz, seq, d_model)


def _attn_kernel(sinks_ref, x_ref, g_ref, w_qkv_ref, b_qkv_ref, w_o_ref, b_o_ref, out_ref,
                 q_s, kv_s, att_s, *, tile, d_model):
    n_blocks = tile // CHUNK
    d_q = ATTN_HEADS * HEADDIM
    t_idx = pl.program_id(1)

    @pl.when(t_idx == 0)
    def _():
        kv_s[:, 0:CHUNK, :] = jnp.zeros((8, CHUNK, LANES), BF16)

    @pl.when(t_idx != 0)
    def _():
        kv_s[:, 0:CHUNK, :] = kv_s[:, tile:tile + CHUNK, :]

    x = x_ref[0]
    y = _rmsnorm(x, g_ref[...]).astype(BF16)
    qkv = _dot(y, w_qkv_ref[...]) + b_qkv_ref[...]
    q_s[...] = (qkv[:, 0:d_q] * (HEADDIM ** -0.5)).astype(BF16)

    lo_lane = lax.broadcasted_iota(jnp.int32, (1, LANES), 1) < HALF
    for base, off in ((0, d_q), (4, d_q + LANES)):
        cur = qkv[:, off:off + LANES]
        swp = pltpu.roll(cur, HALF, axis=1)
        variants = (
            jnp.where(lo_lane, cur, 0.0),
            jnp.where(lo_lane, 0.0, swp),
            jnp.where(lo_lane, swp, 0.0),
            jnp.where(lo_lane, 0.0, cur),
        )
        for i, val in enumerate(variants):
            kv_s[base + i, CHUNK:CHUNK + tile, :] = val.astype(BF16)

    qi = lax.broadcasted_iota(jnp.int32, (CHUNK, 2 * CHUNK), 0)
    si = lax.broadcasted_iota(jnp.int32, (CHUNK, 2 * CHUNK), 1)
    valid = (si > qi) & (si <= qi + CHUNK)
    neg_inf = jnp.float32(-jnp.inf)
    bias = jnp.where(valid, 0.0, neg_inf)
    bias_first = jnp.where(valid & (si >= CHUNK), 0.0, neg_inf)

    def block_body(i, carry):
        qr = pl.ds(pl.multiple_of(i * CHUNK, CHUNK), CHUNK)
        br = pl.ds(pl.multiple_of(i * CHUNK, CHUNK), 2 * CHUNK)
        first = jnp.logical_and(t_idx == 0, i == 0)
        blk_bias = jnp.where(first, bias_first, bias)
        for j in range(ATTN_PAIRS):
            kv = j // PAIRS_PER_KV
            qp = q_s[qr, j * LANES:(j + 1) * LANES]
            o_acc = None
            for par in range(2):
                head = 2 * j + par
                s = _dot_nt(qp, kv_s[2 * kv + par, br, :]) + blk_bias
                sink = sinks_ref[head]
                m = jnp.maximum(jnp.max(s, axis=-1, keepdims=True), sink)
                p = jnp.exp(s - m)
                denom = jnp.sum(p, axis=-1, keepdims=True) + jnp.exp(sink - m)
                probs = (p * (1.0 / denom)).astype(BF16)
                o = _dot(probs, kv_s[4 + 2 * kv + par, br, :])
                o_acc = o if o_acc is None else o_acc + o
            att_s[qr, j * LANES:(j + 1) * LANES] = o_acc.astype(BF16)
        return carry

    lax.fori_loop(0, n_blocks, block_body, 0)

    out_ref[0] = x + _dot(att_s[...], w_o_ref[...]) + b_o_ref[...]


def _attn(h, g, w_qkv, b_qkv, w_o, b_o, sinks, *, tile=256):
    bsz, seq, d_model = h.shape
    d_q = ATTN_HEADS * HEADDIM
    qkv_dim = d_q + 2 * ATTN_KV_HEADS * HEADDIM
    assert w_qkv.shape == (d_model, qkv_dim) and ATTN_KV_HEADS * HEADDIM == LANES
    assert seq % tile == 0 and tile % CHUNK == 0
    const = lambda shape: pl.BlockSpec(shape, lambda b, t, s: (0,) * len(shape))
    kern = functools.partial(_attn_kernel, tile=tile, d_model=d_model)
    grid_spec = pltpu.PrefetchScalarGridSpec(
        num_scalar_prefetch=1,
        grid=(bsz, seq // tile),
        in_specs=[
            pl.BlockSpec((1, tile, d_model), lambda b, t, s: (b, t, 0)),
            const((1, d_model)),
            const((d_model, qkv_dim)),
            const((1, qkv_dim)),
            const((d_q, d_model)),
            const((1, d_model)),
        ],
        out_specs=pl.BlockSpec((1, tile, d_model), lambda b, t, s: (b, t, 0)),
        scratch_shapes=[
            pltpu.VMEM((tile, d_q), BF16),
            pltpu.VMEM((8, tile + CHUNK, LANES), BF16),
            pltpu.VMEM((tile, d_q), BF16),
        ],
    )
    return pl.pallas_call(
        kern,
        grid_spec=grid_spec,
        out_shape=jax.ShapeDtypeStruct(h.shape, F32),
        compiler_params=pltpu.CompilerParams(
            dimension_semantics=("arbitrary", "arbitrary"),
            vmem_limit_bytes=VMEM_LIMIT),
        name="attn",
    )(sinks.astype(F32), h, g.reshape(1, -1).astype(F32), w_qkv.astype(BF16),
      b_qkv.reshape(1, -1).astype(F32), w_o.astype(BF16), b_o.reshape(1, -1).astype(F32))


def kernel(x, norm_mix_g, norm_mlp_g, final_norm_g, w_in_even, w_out_even, gm_ln_g, gm_ln_b,
           gm_w_s, gm_b_s, ssm_conv_w, ssm_conv_b, ssm_dt_bias, ssm_a_log, ssm_d, ssm_norm_g,
           w_qkv, b_qkv, w_o, b_o, attn_sinks, w_up, w_down):
    depth = norm_mix_g.shape[0]
    h = x
    for i in range(depth):
        j = i // 2
        if i % 2 == 0:
            h = _mixer0(h, norm_mix_g[i], w_in_even[j], w_out_even[j], gm_ln_g[j], gm_ln_b[j],
                        gm_w_s[j], gm_b_s[j], ssm_conv_w[j], ssm_conv_b[j], ssm_dt_bias[j],
                        ssm_a_log[j], ssm_d[j], ssm_norm_g[j])
        else:
            h = _attn(h, norm_mix_g[i], w_qkv[j], b_qkv[j], w_o[j], b_o[j], attn_sinks[j])
        h = _mlp(h, norm_mlp_g[i], w_up[i], w_down[i], final_norm_g,
                 final_norm=(i == depth - 1))
    return h
```

```python
import functools

import jax
import jax.numpy as jnp
from jax import lax
from jax.experimental import pallas as pl
from jax.experimental.pallas import tpu as pltpu

F32 = jnp.float32
BF16 = jnp.bfloat16

RMS_EPS = 1e-5
LN_EPS = 1e-5

LANES = 128
CHUNK = 128
HEADDIM = 64
HALF = LANES // 2
LOG2E = 1.4426950408889634

GM_GROUPS = 8
SSM_HEADS = 16
SSM_GROUPS = 4
SSM_STATE = 128
SSM_CONV = 4
CONV_CARRY = 8

ATTN_HEADS = 16
ATTN_KV_HEADS = 2
ATTN_PAIRS = ATTN_HEADS // 2
PAIRS_PER_KV = ATTN_PAIRS // ATTN_KV_HEADS

VMEM_LIMIT = 56 * 1024 * 1024


def _dot(a, b):
    return jnp.dot(a, b, preferred_element_type=F32)


def _dot_nt(a, b):
    return lax.dot_general(a, b, (((1,), (1,)), ((), ())), preferred_element_type=F32)


def _dot_exact(a, b):
    return jnp.dot(a, b, preferred_element_type=F32, precision=lax.Precision.HIGHEST)


def _rmsnorm(x, g):
    ms = jnp.mean(x * x, axis=-1, keepdims=True)
    return x * lax.rsqrt(ms + RMS_EPS) * g


def _gelu_tanh(x):
    c = 0.7978845608028654
    return 0.5 * x * (1.0 + jnp.tanh(c * (x + 0.044715 * (x * x * x))))


def _sigmoid(x):
    return 1.0 / (1.0 + jnp.exp(-x))


def _softplus(x):
    return jnp.maximum(x, 0.0) + jnp.log1p(jnp.exp(-jnp.abs(x)))


def _mixer0_kernel(x_ref, g_ref, w_in_ref, w_out_ref, lng_ref, lnb_ref, ws_ref, bexp_ref,
                   convw_ref, convb_ref, dtb_ref, alog_ref, dexp_ref, ng_ref, hexp_ref,
                   out_ref,
                   xpad_s, xs_s, xlo_s, xhi_s, b_s, c_s, dt_s, a_s, y_s, mix_s, state_s,
                   *, tile, d_model, d_inner):
    n_chunks = tile // CHUNK
    d_bc = SSM_GROUPS * SSM_STATE

    @pl.when(pl.program_id(1) == 0)
    def _():
        state_s[...] = jnp.zeros_like(state_s)
        xpad_s[0:CONV_CARRY, :] = jnp.zeros((CONV_CARRY, xpad_s.shape[1]), F32)

    x = x_ref[0]
    y = _rmsnorm(x, g_ref[...]).astype(BF16)

    row = lax.broadcasted_iota(jnp.int32, (CHUNK, CHUNK), 0)
    col = lax.broadcasted_iota(jnp.int32, (CHUNK, CHUNK), 1)
    causal = row >= col

    o_u, o_v, o_z, o_x = 0, d_model, 2 * d_model, 2 * d_model + d_inner
    o_dt = o_x + d_inner + 2 * d_bc
    u = _gelu_tanh(_dot(y, w_in_ref[:, o_u:o_u + d_model]))
    v = _gelu_tanh(_dot(y, w_in_ref[:, o_v:o_v + d_model]))
    mu = jnp.mean(v, axis=-1, keepdims=True)
    vc = v - mu
    var = jnp.mean(vc * vc, axis=-1, keepdims=True)
    vn = (vc * lax.rsqrt(var + LN_EPS) * lng_ref[...] + lnb_ref[...]).astype(BF16)
    for g in range(GM_GROUPS):
        gs = slice(g * CHUNK, (g + 1) * CHUNK)
        w_g = jnp.where(causal, ws_ref[g], 0.0).astype(BF16)
        for c in range(n_chunks):
            cs = slice(c * CHUNK, (c + 1) * CHUNK)
            mixed = _dot(w_g, vn[cs, gs]) + bexp_ref[:, gs]
            mix_s[cs, gs] = (u[cs, gs] * mixed).astype(BF16)

    z = _dot(y, w_in_ref[:, o_z:o_z + d_inner])
    xpad_s[CONV_CARRY:CONV_CARRY + tile, :] = _dot(y, w_in_ref[:, o_x:o_dt])
    conv = convb_ref[...]
    for k in range(SSM_CONV):
        off = CONV_CARRY - (SSM_CONV - 1) + k
        conv = conv + convw_ref[k:k + 1, :] * xpad_s[off:off + tile, :]
    xpad_s[0:CONV_CARRY, :] = xpad_s[tile:tile + CONV_CARRY, :]
    xbc = conv * _sigmoid(conv)
    xs = xbc[:, 0:d_inner]
    xs_s[...] = xs
    lo_full = lax.broadcasted_iota(jnp.int32, (1, d_inner), 1) % LANES < HALF
    xlo_s[...] = jnp.where(lo_full, xs, 0.0).astype(BF16)
    xhi_s[...] = jnp.where(lo_full, 0.0, xs).astype(BF16)
    b_s[...] = xbc[:, d_inner:d_inner + d_bc]
    c_s[...] = xbc[:, d_inner + d_bc:d_inner + 2 * d_bc]
    dt = _softplus(_dot(y, w_in_ref[:, o_dt:o_dt + LANES]) + dtb_ref[...])
    dt_s[...] = dt
    a_s[...] = dt * (-jnp.exp(alog_ref[...]))

    tril = jnp.where(causal, 1.0, 0.0).astype(F32)

    def chunk_body(c, carry):
        r = pl.ds(pl.multiple_of(c * CHUNK, CHUNK), CHUNK)
        dtc = dt_s[r, :]
        acum = _dot_exact(tril, a_s[r, :])
        acum_t = acum.T
        dt_t = dtc.T
        last = acum[CHUNK - 1:CHUNK, :]
        te_t = (jnp.exp(last - acum) * dtc).T
        decay8 = _dot_exact(jnp.exp(acum[CHUNK - 8:CHUNK, :]), hexp_ref[...])
        chunk_decay = decay8[7:8, :]

        state = state_s[...]
        st_lo = jnp.where(lo_full, state, 0.0).astype(BF16)
        st_hi = jnp.where(lo_full, 0.0, state).astype(BF16)

        y_pairs = []
        s_pairs = []
        pairs_per_group = SSM_HEADS // SSM_GROUPS // 2
        for j in range(SSM_HEADS // 2):
            ps = slice(j * LANES, (j + 1) * LANES)
            if j % pairs_per_group == 0:
                g = j // pairs_per_group
                ns = slice(g * SSM_STATE, (g + 1) * SSM_STATE)
                bg = b_s[r, ns]
                cg = c_s[r, ns]
                cb = _dot_nt(cg.astype(BF16), bg.astype(BF16))
                bg_t = bg.T
            y_acc = None
            s_acc = None
            for h, x_sel, st_sel in ((2 * j, xlo_s, st_lo), (2 * j + 1, xhi_s, st_hi)):
                a_col = jnp.broadcast_to(acum[:, h:h + 1], (CHUNK, CHUNK))
                seg = a_col - acum_t[h:h + 1, :]
                decay = jnp.exp(jnp.where(causal, seg, -jnp.inf))
                w_intra = (cb * decay * dt_t[h:h + 1, :]).astype(BF16)
                c_in = (cg * jnp.exp(a_col)).astype(BF16)
                xh = x_sel[r, ps]
                lhs = jnp.concatenate([w_intra, c_in], axis=1)
                rhs = jnp.concatenate([xh, st_sel[:, ps]], axis=0)
                yh = _dot(lhs, rhs)
                sh = _dot((bg_t * te_t[h:h + 1, :]).astype(BF16), xh)
                y_acc = yh if y_acc is None else y_acc + yh
                s_acc = sh if s_acc is None else s_acc + sh
            y_pairs.append(y_acc)
            s_pairs.append(s_acc)
        y_s[r, :] = jnp.concatenate(y_pairs, axis=1)
        state_s[...] = state * chunk_decay + jnp.concatenate(s_pairs, axis=1)
        return carry

    lax.fori_loop(0, n_chunks, chunk_body, 0)

    yv = y_s[...] + xs_s[...] * dexp_ref[...]
    yv = yv * (z * _sigmoid(z))
    gw = d_inner // SSM_GROUPS
    for g in range(SSM_GROUPS):
        gs = slice(g * gw, (g + 1) * gw)
        mix_s[:, d_model + g * gw:d_model + (g + 1) * gw] = _rmsnorm(
            yv[:, gs], ng_ref[:, gs]).astype(BF16)

    out_ref[0] = x + _dot(mix_s[...], w_out_ref[...])


def _mixer0(h, g, w_in, w_out, ln_g, ln_b, w_s, b_s, conv_w, conv_b, dt_bias, a_log, d_skip,
            norm_g, *, tile=256):
    bsz, seq, d_model = h.shape
    d_inner = d_model
    d_bc = SSM_GROUPS * SSM_STATE
    conv_dim = d_inner + 2 * d_bc
    n_main = 3 * d_model + conv_dim
    assert w_in.shape == (d_model, n_main + SSM_HEADS)
    assert seq % tile == 0 and tile % CHUNK == 0 and d_inner == SSM_HEADS * HEADDIM

    pad = LANES - SSM_HEADS
    w_in_p = jnp.pad(w_in, ((0, 0), (0, pad))).astype(BF16)
    row = lambda v: v.reshape(1, -1).astype(F32)
    pad_row = lambda v: jnp.pad(v.astype(F32), (0, pad)).reshape(1, LANES)
    bexp = jnp.repeat(b_s.T.astype(F32), CHUNK, axis=1)
    dexp = jnp.repeat(d_skip.astype(F32), HEADDIM).reshape(1, d_inner)
    hexp = (jnp.arange(LANES)[:, None] == (jnp.arange(d_inner)[None, :] // HEADDIM)).astype(F32)

    const = lambda shape: pl.BlockSpec(shape, lambda b, t: (0,) * len(shape))
    kern = functools.partial(_mixer0_kernel, tile=tile, d_model=d_model, d_inner=d_inner)
    return pl.pallas_call(
        kern,
        grid=(bsz, seq // tile),
        in_specs=[
            pl.BlockSpec((1, tile, d_model), lambda b, t: (b, t, 0)),
            const((1, d_model)),
            const(w_in_p.shape),
            const(w_out.shape),
            const((1, d_model)), const((1, d_model)),
            const(w_s.shape),
            const(bexp.shape),
            const(conv_w.shape), const((1, conv_dim)),
            const((1, LANES)), const((1, LANES)),
            const((1, d_inner)), const((1, d_inner)),
            const(hexp.shape),
        ],
        out_specs=pl.BlockSpec((1, tile, d_model), lambda b, t: (b, t, 0)),
        out_shape=jax.ShapeDtypeStruct(h.shape, F32),
        scratch_shapes=[
            pltpu.VMEM((tile + CONV_CARRY, conv_dim), F32),
            pltpu.VMEM((tile, d_inner), F32),
            pltpu.VMEM((tile, d_inner), BF16),
            pltpu.VMEM((tile, d_inner), BF16),
            pltpu.VMEM((tile, d_bc), F32),
            pltpu.VMEM((tile, d_bc), F32),
            pltpu.VMEM((tile, LANES), F32),
            pltpu.VMEM((tile, LANES), F32),
            pltpu.VMEM((tile, d_inner), F32),
            pltpu.VMEM((tile, d_model + d_inner), BF16),
            pltpu.VMEM((SSM_STATE, d_inner), F32),
        ],
        compiler_params=pltpu.CompilerParams(
            dimension_semantics=("arbitrary", "arbitrary"),
            vmem_limit_bytes=VMEM_LIMIT),
        name="mixer0",
    )(h, row(g), w_in_p, w_out.astype(BF16), row(ln_g), row(ln_b), w_s.astype(F32), bexp,
      conv_w.astype(F32), row(conv_b), pad_row(dt_bias), pad_row(a_log), dexp, row(norm_g),
      hexp)


def _mlp_kernel(x_ref, g_ref, w_up_ref, w_down_ref, gf_ref, out_ref, *, ff_chunk, final_norm):
    x = x_ref[...]
    y = _rmsnorm(x, g_ref[...]).astype(BF16)
    acc = x
    d_ff = w_up_ref.shape[1]
    for c in range(d_ff // ff_chunk):
        cs = slice(c * ff_chunk, (c + 1) * ff_chunk)
        hid = jnp.maximum(_dot(y, w_up_ref[:, cs]), 0.0)
        acc = acc + _dot((hid * hid).astype(BF16), w_down_ref[cs, :])
    if final_norm:
        acc = _rmsnorm(acc, gf_ref[...])
    out_ref[...] = acc


def _mlp(h, g, w_up, w_down, g_final, *, final_norm, tile=512, ff_chunk=1024):
    bsz, seq, d_model = h.shape
    n = bsz * seq
    d_ff = w_up.shape[1]
    assert n % tile == 0 and d_ff % ff_chunk == 0
    x2 = h.reshape(n, d_model)
    const = lambda shape: pl.BlockSpec(shape, lambda i: (0,) * len(shape))
    out = pl.pallas_call(
        functools.partial(_mlp_kernel, ff_chunk=ff_chunk, final_norm=final_norm),
        grid=(n // tile,),
        in_specs=[
            pl.BlockSpec((tile, d_model), lambda i: (i, 0)),
            const((1, d_model)),
            const((d_model, d_ff)),
            const((d_ff, d_model)),
            const((1, d_model)),
        ],
        out_specs=pl.BlockSpec((tile, d_model), lambda i: (i, 0)),
        out_shape=jax.ShapeDtypeStruct((n, d_model), F32),
        compiler_params=pltpu.CompilerParams(
            dimension_semantics=("arbitrary",), vmem_limit_bytes=VMEM_LIMIT),
        name="mlp_final" if final_norm else "mlp",
    )(x2, g.reshape(1, -1).astype(F32), w_up.astype(BF16), w_down.astype(BF16),
      g_final.reshape(1, -1).astype(F32))
    return out.reshape(bsz, seq, d_model)


def _attn_kernel(sinks_ref, x_ref, g_ref, wq_ref, bq_ref, wk_ref, bk_ref, wvt_ref, bvt_ref,
                 wo_ref, bo_ref, out_ref, q_s, k_s, vt_s, att_s, *, tile):
    n_blocks = tile // CHUNK
    grp_rows = PAIRS_PER_KV * CHUNK
    t_idx = pl.program_id(1)

    @pl.when(t_idx == 0)
    def _():
        k_s[:, 0:CHUNK, :] = jnp.zeros((4, CHUNK, LANES), BF16)
        vt_s[:, 0] = jnp.zeros((4, LANES, CHUNK), BF16)

    @pl.when(t_idx != 0)
    def _():
        k_s[:, 0:CHUNK, :] = k_s[:, tile:tile + CHUNK, :]
        vt_s[:, 0] = vt_s[:, n_blocks]

    x = x_ref[0]
    y = _rmsnorm(x, g_ref[...]).astype(BF16)
    q = ((_dot(y, wq_ref[...]) + bq_ref[...]) * (HEADDIM ** -0.5 * LOG2E)).astype(BF16)
    for i in range(n_blocks):
        for j in range(ATTN_PAIRS):
            q_s[i, j * CHUNK:(j + 1) * CHUNK, :] = q[i * CHUNK:(i + 1) * CHUNK,
                                                     j * LANES:(j + 1) * LANES]

    k = _dot(y, wk_ref[...]) + bk_ref[...]
    k_swp = pltpu.roll(k, HALF, axis=1)
    lo_lane = lax.broadcasted_iota(jnp.int32, (1, LANES), 1) < HALF
    k_variants = (jnp.where(lo_lane, k, 0.0), jnp.where(lo_lane, 0.0, k_swp),
                  jnp.where(lo_lane, k_swp, 0.0), jnp.where(lo_lane, 0.0, k))
    for g, val in enumerate(k_variants):
        k_s[g, CHUNK:CHUNK + tile, :] = val.astype(BF16)

    bvt = jnp.concatenate([bvt_ref[...]] * n_blocks, axis=1)
    vt = _dot_nt(wvt_ref[...], y) + bvt
    vt_swp = pltpu.roll(vt, HALF, axis=0)
    lo_row = lax.broadcasted_iota(jnp.int32, (LANES, 1), 0) < HALF
    v_variants = (jnp.where(lo_row, vt, 0.0), jnp.where(lo_row, 0.0, vt_swp),
                  jnp.where(lo_row, vt_swp, 0.0), jnp.where(lo_row, 0.0, vt))
    for g, val in enumerate(v_variants):
        for b in range(n_blocks):
            vt_s[g, 1 + b] = val[:, b * CHUNK:(b + 1) * CHUNK].astype(BF16)

    si = lax.broadcasted_iota(jnp.int32, (2 * CHUNK, CHUNK), 0)
    qi = lax.broadcasted_iota(jnp.int32, (2 * CHUNK, CHUNK), 1)
    valid = (si > qi) & (si <= qi + CHUNK)
    neg_inf = jnp.float32(-jnp.inf)
    bias = jnp.where(valid, 0.0, neg_inf)
    bias_first = jnp.where(valid & (si >= CHUNK), 0.0, neg_inf)

    def block_body(i, carry):
        br = pl.ds(pl.multiple_of(i * CHUNK, CHUNK), 2 * CHUNK)
        qr = pl.ds(pl.multiple_of(i * CHUNK, CHUNK), CHUNK)
        first = jnp.logical_and(t_idx == 0, i == 0)
        blk_bias = jnp.where(first, bias_first, bias)
        for kv in range(ATTN_KV_HEADS):
            q4 = q_s[i, kv * grp_rows:(kv + 1) * grp_rows, :]
            o_acc = None
            for par in range(2):
                g = 2 * kv + par
                st_all = _dot_nt(k_s[g, br, :], q4)
                p_cols = []
                inv_cols = []
                for jj in range(PAIRS_PER_KV):
                    head = 2 * (kv * PAIRS_PER_KV + jj) + par
                    s = st_all[:, jj * CHUNK:(jj + 1) * CHUNK] + blk_bias
                    sink = sinks_ref[head] * LOG2E
                    m = jnp.maximum(jnp.max(s, axis=0, keepdims=True), sink)
                    p = jnp.exp2(s - m)
                    denom = jnp.sum(p, axis=0, keepdims=True) + jnp.exp2(sink - m)
                    p_cols.append(p.astype(BF16))
                    inv_cols.append(1.0 / denom)
                v_band = jnp.concatenate([vt_s[g, i], vt_s[g, i + 1]], axis=1)
                ot = _dot(v_band, jnp.concatenate(p_cols, axis=1))
                ot = ot * jnp.concatenate(inv_cols, axis=1)
                o_acc = ot if o_acc is None else o_acc + ot
            for jj in range(PAIRS_PER_KV):
                j = kv * PAIRS_PER_KV + jj
                att_s[qr, j * LANES:(j + 1) * LANES] = (
                    o_acc[:, jj * CHUNK:(jj + 1) * CHUNK].T.astype(BF16))
        return carry

    lax.fori_loop(0, n_blocks, block_body, 0)

    out_ref[0] = x + _dot(att_s[...], wo_ref[...]) + bo_ref[...]


def _attn(h, g, w_qkv, b_qkv, w_o, b_o, sinks, *, tile=256):
    bsz, seq, d_model = h.shape
    d_q = ATTN_HEADS * HEADDIM
    d_kv = ATTN_KV_HEADS * HEADDIM
    assert w_qkv.shape == (d_model, d_q + 2 * d_kv) and d_kv == LANES
    assert seq % tile == 0 and tile % CHUNK == 0
    n_blocks = tile // CHUNK
    row = lambda v: v.reshape(1, -1).astype(F32)
    w_q, w_k, w_v = w_qkv[:, :d_q], w_qkv[:, d_q:d_q + d_kv], w_qkv[:, d_q + d_kv:]
    b_q, b_k, b_v = b_qkv[:d_q], b_qkv[d_q:d_q + d_kv], b_qkv[d_q + d_kv:]
    b_vt = jnp.broadcast_to(b_v.astype(F32)[:, None], (d_kv, CHUNK))
    const = lambda shape: pl.BlockSpec(shape, lambda b, t, s: (0,) * len(shape))
    grid_spec = pltpu.PrefetchScalarGridSpec(
        num_scalar_prefetch=1,
        grid=(bsz, seq // tile),
        in_specs=[
            pl.BlockSpec((1, tile, d_model), lambda b, t, s: (b, t, 0)),
            const((1, d_model)),
            const((d_model, d_q)), const((1, d_q)),
            const((d_model, d_kv)), const((1, d_kv)),
            const((d_kv, d_model)), const((d_kv, CHUNK)),
            const((d_q, d_model)), const((1, d_model)),
        ],
        out_specs=pl.BlockSpec((1, tile, d_model), lambda b, t, s: (b, t, 0)),
        scratch_shapes=[
            pltpu.VMEM((n_blocks, ATTN_PAIRS * CHUNK, LANES), BF16),
            pltpu.VMEM((4, tile + CHUNK, LANES), BF16),
            pltpu.VMEM((4, n_blocks + 1, LANES, CHUNK), BF16),
            pltpu.VMEM((tile, d_q), BF16),
        ],
    )
    return pl.pallas_call(
        functools.partial(_attn_kernel, tile=tile),
        grid_spec=grid_spec,
        out_shape=jax.ShapeDtypeStruct(h.shape, F32),
        compiler_params=pltpu.CompilerParams(
            dimension_semantics=("arbitrary", "arbitrary"),
            vmem_limit_bytes=VMEM_LIMIT),
        name="attn",
    )(sinks.astype(F32), h, row(g), w_q.astype(BF16), row(b_q), w_k.astype(BF16), row(b_k),
      w_v.T.astype(BF16), b_vt, w_o.astype(BF16), row(b_o))


def kernel(x, norm_mix_g, norm_mlp_g, final_norm_g, w_in_even, w_out_even, gm_ln_g, gm_ln_b,
           gm_w_s, gm_b_s, ssm_conv_w, ssm_conv_b, ssm_dt_bias, ssm_a_log, ssm_d, ssm_norm_g,
           w_qkv, b_qkv, w_o, b_o, attn_sinks, w_up, w_down):
    depth = norm_mix_g.shape[0]
    h = x
    for i in range(depth):
        j = i // 2
        if i % 2 == 0:
            h = _mixer0(h, norm_mix_g[i], w_in_even[j], w_out_even[j], gm_ln_g[j], gm_ln_b[j],
                        gm_w_s[j], gm_b_s[j], ssm_conv_w[j], ssm_conv_b[j], ssm_dt_bias[j],
                        ssm_a_log[j], ssm_d[j], ssm_norm_g[j])
        else:
            h = _attn(h, norm_mix_g[i], w_qkv[j], b_qkv[j], w_o[j], b_o[j], attn_sinks[j])
        h = _mlp(h, norm_mlp_g[i], w_up[i], w_down[i], final_norm_g,
                 final_norm=(i == depth - 1))
    return h
```

```python
import functools

import jax
import jax.numpy as jnp
from jax import lax
from jax.experimental import pallas as pl
from jax.experimental.pallas import tpu as pltpu

F32 = jnp.float32
BF16 = jnp.bfloat16

RMS_EPS = 1e-5
LN_EPS = 1e-5

LANES = 128
CHUNK = 128
HEADDIM = 64
HALF = LANES // 2
LOG2E = 1.4426950408889634

GM_GROUPS = 8
SSM_HEADS = 16
SSM_GROUPS = 4
SSM_STATE = 128
SSM_CONV = 4
CONV_CARRY = 8

ATTN_HEADS = 16
ATTN_KV_HEADS = 2
ATTN_PAIRS = ATTN_HEADS // 2
PAIRS_PER_KV = ATTN_PAIRS // ATTN_KV_HEADS

VMEM_LIMIT = 56 * 1024 * 1024


def _dot(a, b):
    return jnp.dot(a, b, preferred_element_type=F32)


def _dot_nt(a, b):
    return lax.dot_general(a, b, (((1,), (1,)), ((), ())), preferred_element_type=F32)


def _dot_exact(a, b):
    return jnp.dot(a, b, preferred_element_type=F32, precision=lax.Precision.HIGHEST)


def _rmsnorm(x, g):
    ms = jnp.mean(x * x, axis=-1, keepdims=True)
    return x * lax.rsqrt(ms + RMS_EPS) * g


def _gelu_tanh(x):
    c = 0.7978845608028654
    return 0.5 * x * (1.0 + jnp.tanh(c * (x + 0.044715 * (x * x * x))))


def _sigmoid(x):
    return 1.0 / (1.0 + jnp.exp(-x))


def _softplus(x):
    return jnp.maximum(x, 0.0) + jnp.log1p(jnp.exp(-jnp.abs(x)))


def _mixer0_kernel(x_ref, g_ref, w_in_ref, w_out_ref, lng_ref, lnb_ref, ws_ref, bexp_ref,
                   convw_ref, convb_ref, dtb_ref, alog_ref, dexp_ref, ng_ref, hexp_ref,
                   out_ref,
                   xpad_s, xs_s, xlo_s, xhi_s, b_s, c_s, dt_s, a_s, y_s, mix_s, state_s,
                   *, tile, d_model, d_inner):
    n_chunks = tile // CHUNK
    d_bc = SSM_GROUPS * SSM_STATE

    @pl.when(pl.program_id(1) == 0)
    def _():
        state_s[...] = jnp.zeros_like(state_s)
        xpad_s[0:CONV_CARRY, :] = jnp.zeros((CONV_CARRY, xpad_s.shape[1]), F32)

    x = x_ref[0]
    y = _rmsnorm(x, g_ref[...]).astype(BF16)

    row = lax.broadcasted_iota(jnp.int32, (CHUNK, CHUNK), 0)
    col = lax.broadcasted_iota(jnp.int32, (CHUNK, CHUNK), 1)
    causal = row >= col

    o_u, o_v, o_z, o_x = 0, d_model, 2 * d_model, 2 * d_model + d_inner
    o_dt = o_x + d_inner + 2 * d_bc
    u = _gelu_tanh(_dot(y, w_in_ref[:, o_u:o_u + d_model]))
    v = _gelu_tanh(_dot(y, w_in_ref[:, o_v:o_v + d_model]))
    mu = jnp.mean(v, axis=-1, keepdims=True)
    vc = v - mu
    var = jnp.mean(vc * vc, axis=-1, keepdims=True)
    vn = (vc * lax.rsqrt(var + LN_EPS) * lng_ref[...] + lnb_ref[...]).astype(BF16)
    for g in range(GM_GROUPS):
        gs = slice(g * CHUNK, (g + 1) * CHUNK)
        w_g = jnp.where(causal, ws_ref[g], 0.0).astype(BF16)
        for c in range(n_chunks):
            cs = slice(c * CHUNK, (c + 1) * CHUNK)
            mixed = _dot(w_g, vn[cs, gs]) + bexp_ref[:, gs]
            mix_s[cs, gs] = (u[cs, gs] * mixed).astype(BF16)

    z = _dot(y, w_in_ref[:, o_z:o_z + d_inner])
    xpad_s[CONV_CARRY:CONV_CARRY + tile, :] = _dot(y, w_in_ref[:, o_x:o_dt])
    conv = convb_ref[...]
    for k in range(SSM_CONV):
        off = CONV_CARRY - (SSM_CONV - 1) + k
        conv = conv + convw_ref[k:k + 1, :] * xpad_s[off:off + tile, :]
    xpad_s[0:CONV_CARRY, :] = xpad_s[tile:tile + CONV_CARRY, :]
    xbc = conv * _sigmoid(conv)
    xs = xbc[:, 0:d_inner]
    xs_s[...] = xs
    lo_full = lax.broadcasted_iota(jnp.int32, (1, d_inner), 1) % LANES < HALF
    xlo_s[...] = jnp.where(lo_full, xs, 0.0).astype(BF16)
    xhi_s[...] = jnp.where(lo_full, 0.0, xs).astype(BF16)
    b_s[...] = xbc[:, d_inner:d_inner + d_bc]
    c_s[...] = xbc[:, d_inner + d_bc:d_inner + 2 * d_bc]
    dt = _softplus(_dot(y, w_in_ref[:, o_dt:o_dt + LANES]) + dtb_ref[...])
    dt_s[...] = dt
    a_s[...] = dt * (-jnp.exp(alog_ref[...]))

    tril = jnp.where(causal, 1.0, 0.0).astype(F32)

    def chunk_body(c, carry):
        r = pl.ds(pl.multiple_of(c * CHUNK, CHUNK), CHUNK)
        dtc = dt_s[r, :]
        acum = _dot_exact(tril, a_s[r, :])
        acum_t = acum.T
        dt_t = dtc.T
        last = acum[CHUNK - 1:CHUNK, :]
        te_t = (jnp.exp(last - acum) * dtc).T
        decay8 = _dot_exact(jnp.exp(acum[CHUNK - 8:CHUNK, :]), hexp_ref[...])
        chunk_decay = decay8[7:8, :]

        state = state_s[...]
        st_lo = jnp.where(lo_full, state, 0.0).astype(BF16)
        st_hi = jnp.where(lo_full, 0.0, state).astype(BF16)

        y_pairs = []
        s_pairs = []
        pairs_per_group = SSM_HEADS // SSM_GROUPS // 2
        for j in range(SSM_HEADS // 2):
            ps = slice(j * LANES, (j + 1) * LANES)
            if j % pairs_per_group == 0:
                g = j // pairs_per_group
                ns = slice(g * SSM_STATE, (g + 1) * SSM_STATE)
                bg = b_s[r, ns]
                cg = c_s[r, ns]
                cb = _dot_nt(cg.astype(BF16), bg.astype(BF16))
                bg_t = bg.T
            y_acc = None
            s_acc = None
            for h, x_sel, st_sel in ((2 * j, xlo_s, st_lo), (2 * j + 1, xhi_s, st_hi)):
                a_col = jnp.broadcast_to(acum[:, h:h + 1], (CHUNK, CHUNK))
                seg = a_col - acum_t[h:h + 1, :]
                decay = jnp.exp(jnp.where(causal, seg, -jnp.inf))
                w_intra = (cb * decay * dt_t[h:h + 1, :]).astype(BF16)
                c_in = (cg * jnp.exp(a_col)).astype(BF16)
                xh = x_sel[r, ps]
                lhs = jnp.concatenate([w_intra, c_in], axis=1)
                rhs = jnp.concatenate([xh, st_sel[:, ps]], axis=0)
                yh = _dot(lhs, rhs)
                sh = _dot((bg_t * te_t[h:h + 1, :]).astype(BF16), xh)
                y_acc = yh if y_acc is None else y_acc + yh
                s_acc = sh if s_acc is None else s_acc + sh
            y_pairs.append(y_acc)
            s_pairs.append(s_acc)
        y_s[r, :] = jnp.concatenate(y_pairs, axis=1)
        state_s[...] = state * chunk_decay + jnp.concatenate(s_pairs, axis=1)
        return carry

    lax.fori_loop(0, n_chunks, chunk_body, 0)

    yv = y_s[...] + xs_s[...] * dexp_ref[...]
    yv = yv * (z * _sigmoid(z))
    gw = d_inner // SSM_GROUPS
    for g in range(SSM_GROUPS):
        gs = slice(g * gw, (g + 1) * gw)
        mix_s[:, d_model + g * gw:d_model + (g + 1) * gw] = _rmsnorm(
            yv[:, gs], ng_ref[:, gs]).astype(BF16)

    out_ref[0] = x + _dot(mix_s[...], w_out_ref[...])


def _mixer0(h, g, w_in, w_out, ln_g, ln_b, w_s, b_s, conv_w, conv_b, dt_bias, a_log, d_skip,
            norm_g, *, tile=256):
    bsz, seq, d_model = h.shape
    d_inner = d_model
    d_bc = SSM_GROUPS * SSM_STATE
    conv_dim = d_inner + 2 * d_bc
    n_main = 3 * d_model + conv_dim
    assert w_in.shape == (d_model, n_main + SSM_HEADS)
    assert seq % tile == 0 and tile % CHUNK == 0 and d_inner == SSM_HEADS * HEADDIM

    pad = LANES - SSM_HEADS
    w_in_p = jnp.pad(w_in, ((0, 0), (0, pad))).astype(BF16)
    row = lambda v: v.reshape(1, -1).astype(F32)
    pad_row = lambda v: jnp.pad(v.astype(F32), (0, pad)).reshape(1, LANES)
    bexp = jnp.repeat(b_s.T.astype(F32), CHUNK, axis=1)
    dexp = jnp.repeat(d_skip.astype(F32), HEADDIM).reshape(1, d_inner)
    hexp = (jnp.arange(LANES)[:, None] == (jnp.arange(d_inner)[None, :] // HEADDIM)).astype(F32)

    const = lambda shape: pl.BlockSpec(shape, lambda b, t: (0,) * len(shape))
    kern = functools.partial(_mixer0_kernel, tile=tile, d_model=d_model, d_inner=d_inner)
    return pl.pallas_call(
        kern,
        grid=(bsz, seq // tile),
        in_specs=[
            pl.BlockSpec((1, tile, d_model), lambda b, t: (b, t, 0)),
            const((1, d_model)),
            const(w_in_p.shape),
            const(w_out.shape),
            const((1, d_model)), const((1, d_model)),
            const(w_s.shape),
            const(bexp.shape),
            const(conv_w.shape), const((1, conv_dim)),
            const((1, LANES)), const((1, LANES)),
            const((1, d_inner)), const((1, d_inner)),
            const(hexp.shape),
        ],
        out_specs=pl.BlockSpec((1, tile, d_model), lambda b, t: (b, t, 0)),
        out_shape=jax.ShapeDtypeStruct(h.shape, F32),
        scratch_shapes=[
            pltpu.VMEM((tile + CONV_CARRY, conv_dim), F32),
            pltpu.VMEM((tile, d_inner), F32),
            pltpu.VMEM((tile, d_inner), BF16),
            pltpu.VMEM((tile, d_inner), BF16),
            pltpu.VMEM((tile, d_bc), F32),
            pltpu.VMEM((tile, d_bc), F32),
            pltpu.VMEM((tile, LANES), F32),
            pltpu.VMEM((tile, LANES), F32),
            pltpu.VMEM((tile, d_inner), F32),
            pltpu.VMEM((tile, d_model + d_inner), BF16),
            pltpu.VMEM((SSM_STATE, d_inner), F32),
        ],
        compiler_params=pltpu.CompilerParams(
            dimension_semantics=("arbitrary", "arbitrary"),
            vmem_limit_bytes=VMEM_LIMIT),
        name="mixer0",
    )(h, row(g), w_in_p, w_out.astype(BF16), row(ln_g), row(ln_b), w_s.astype(F32), bexp,
      conv_w.astype(F32), row(conv_b), pad_row(dt_bias), pad_row(a_log), dexp, row(norm_g),
      hexp)


def _mlp_kernel(x_ref, g_ref, w_up_ref, w_down_ref, gf_ref, out_ref, *, ff_chunk, final_norm):
    x = x_ref[...]
    y = _rmsnorm(x, g_ref[...]).astype(BF16)
    acc = x
    d_ff = w_up_ref.shape[1]
    for c in range(d_ff // ff_chunk):
        cs = slice(c * ff_chunk, (c + 1) * ff_chunk)
        hid = jnp.maximum(_dot(y, w_up_ref[:, cs]), 0.0)
        acc = acc + _dot((hid * hid).astype(BF16), w_down_ref[cs, :])
    if final_norm:
        acc = _rmsnorm(acc, gf_ref[...])
    out_ref[...] = acc


def _mlp(h, g, w_up, w_down, g_final, *, final_norm, tile=512, ff_chunk=1024):
    bsz, seq, d_model = h.shape
    n = bsz * seq
    d_ff = w_up.shape[1]
    assert n % tile == 0 and d_ff % ff_chunk == 0
    x2 = h.reshape(n, d_model)
    const = lambda shape: pl.BlockSpec(shape, lambda i: (0,) * len(shape))
    out = pl.pallas_call(
        functools.partial(_mlp_kernel, ff_chunk=ff_chunk, final_norm=final_norm),
        grid=(n // tile,),
        in_specs=[
            pl.BlockSpec((tile, d_model), lambda i: (i, 0)),
            const((1, d_model)),
            const((d_model, d_ff)),
            const((d_ff, d_model)),
            const((1, d_model)),
        ],
        out_specs=pl.BlockSpec((tile, d_model), lambda i: (i, 0)),
        out_shape=jax.ShapeDtypeStruct((n, d_model), F32),
        compiler_params=pltpu.CompilerParams(
            dimension_semantics=("arbitrary",), vmem_limit_bytes=VMEM_LIMIT),
        name="mlp_final" if final_norm else "mlp",
    )(x2, g.reshape(1, -1).astype(F32), w_up.astype(BF16), w_down.astype(BF16),
      g_final.reshape(1, -1).astype(F32))
    return out.reshape(bsz, seq, d_model)


def _attn_kernel(sinks_ref, x_ref, g_ref, wq_ref, bq_ref, wk_ref, bk_ref, wvt_ref, bvt_ref,
                 wo_ref, bo_ref, out_ref, q_s, k_s, vt_s, att_s, *, tile):
    n_blocks = tile // CHUNK
    grp_rows = PAIRS_PER_KV * CHUNK
    t_idx = pl.program_id(1)

    @pl.when(t_idx == 0)
    def _():
        k_s[:, 0:CHUNK, :] = jnp.zeros((4, CHUNK, LANES), BF16)
        vt_s[:, 0] = jnp.zeros((4, LANES, CHUNK), BF16)

    @pl.when(t_idx != 0)
    def _():
        k_s[:, 0:CHUNK, :] = k_s[:, tile:tile + CHUNK, :]
        vt_s[:, 0] = vt_s[:, n_blocks]

    x = x_ref[0]
    y = _rmsnorm(x, g_ref[...]).astype(BF16)
    q = ((_dot(y, wq_ref[...]) + bq_ref[...]) * (HEADDIM ** -0.5 * LOG2E)).astype(BF16)
    for i in range(n_blocks):
        for j in range(ATTN_PAIRS):
            q_s[i, j * CHUNK:(j + 1) * CHUNK, :] = q[i * CHUNK:(i + 1) * CHUNK,
                                                     j * LANES:(j + 1) * LANES]

    k = _dot(y, wk_ref[...]) + bk_ref[...]
    k_swp = pltpu.roll(k, HALF, axis=1)
    lo_lane = lax.broadcasted_iota(jnp.int32, (1, LANES), 1) < HALF
    k_variants = (jnp.where(lo_lane, k, 0.0), jnp.where(lo_lane, 0.0, k_swp),
                  jnp.where(lo_lane, k_swp, 0.0), jnp.where(lo_lane, 0.0, k))
    for g, val in enumerate(k_variants):
        k_s[g, CHUNK:CHUNK + tile, :] = val.astype(BF16)

    bvt = jnp.concatenate([bvt_ref[...]] * n_blocks, axis=1)
    vt = _dot_nt(wvt_ref[...], y) + bvt
    vt_swp = pltpu.roll(vt, HALF, axis=0)
    lo_row = lax.broadcasted_iota(jnp.int32, (LANES, 1), 0) < HALF
    v_variants = (jnp.where(lo_row, vt, 0.0), jnp.where(lo_row, 0.0, vt_swp),
                  jnp.where(lo_row, vt_swp, 0.0), jnp.where(lo_row, 0.0, vt))
    for g, val in enumerate(v_variants):
        for b in range(n_blocks):
            vt_s[g, 1 + b] = val[:, b * CHUNK:(b + 1) * CHUNK].astype(BF16)

    si = lax.broadcasted_iota(jnp.int32, (2 * CHUNK, CHUNK), 0)
    qi = lax.broadcasted_iota(jnp.int32, (2 * CHUNK, CHUNK), 1)
    valid = (si > qi) & (si <= qi + CHUNK)
    neg_inf = jnp.float32(-jnp.inf)
    bias = jnp.where(valid, 0.0, neg_inf)
    bias_first = jnp.where(valid & (si >= CHUNK), 0.0, neg_inf)

    def block_body(i, carry):
        br = pl.ds(pl.multiple_of(i * CHUNK, CHUNK), 2 * CHUNK)
        qr = pl.ds(pl.multiple_of(i * CHUNK, CHUNK), CHUNK)
        first = jnp.logical_and(t_idx == 0, i == 0)
        blk_bias = jnp.where(first, bias_first, bias)

        def scores_t(g):
            kv = g // 2
            return _dot_nt(k_s[g, br, :], q_s[i, kv * grp_rows:(kv + 1) * grp_rows, :])

        st_next = scores_t(0)
        for kv in range(ATTN_KV_HEADS):
            o_acc = None
            for par in range(2):
                g = 2 * kv + par
                st_all = st_next
                if g + 1 < 2 * ATTN_KV_HEADS:
                    st_next = scores_t(g + 1)
                p_cols = []
                inv_cols = []
                for jj in range(PAIRS_PER_KV):
                    head = 2 * (kv * PAIRS_PER_KV + jj) + par
                    s = st_all[:, jj * CHUNK:(jj + 1) * CHUNK] + blk_bias
                    sink = sinks_ref[head] * LOG2E
                    m = jnp.maximum(jnp.max(s, axis=0, keepdims=True), sink)
                    p = jnp.exp2(s - m)
                    denom = jnp.sum(p, axis=0, keepdims=True) + jnp.exp2(sink - m)
                    p_cols.append(p.astype(BF16))
                    inv_cols.append(1.0 / denom)
                v_band = jnp.concatenate([vt_s[g, i], vt_s[g, i + 1]], axis=1)
                ot = _dot(v_band, jnp.concatenate(p_cols, axis=1))
                ot = ot * jnp.concatenate(inv_cols, axis=1)
                o_acc = ot if o_acc is None else o_acc + ot
            for jj in range(PAIRS_PER_KV):
                j = kv * PAIRS_PER_KV + jj
                att_s[qr, j * LANES:(j + 1) * LANES] = (
                    o_acc[:, jj * CHUNK:(jj + 1) * CHUNK].T.astype(BF16))
        return carry

    lax.fori_loop(0, n_blocks, block_body, 0)

    out_ref[0] = x + _dot(att_s[...], wo_ref[...]) + bo_ref[...]


def _attn(h, g, w_qkv, b_qkv, w_o, b_o, sinks, *, tile=512):
    bsz, seq, d_model = h.shape
    d_q = ATTN_HEADS * HEADDIM
    d_kv = ATTN_KV_HEADS * HEADDIM
    assert w_qkv.shape == (d_model, d_q + 2 * d_kv) and d_kv == LANES
    assert seq % tile == 0 and tile % CHUNK == 0
    n_blocks = tile // CHUNK
    row = lambda v: v.reshape(1, -1).astype(F32)
    w_q, w_k, w_v = w_qkv[:, :d_q], w_qkv[:, d_q:d_q + d_kv], w_qkv[:, d_q + d_kv:]
    b_q, b_k, b_v = b_qkv[:d_q], b_qkv[d_q:d_q + d_kv], b_qkv[d_q + d_kv:]
    b_vt = jnp.broadcast_to(b_v.astype(F32)[:, None], (d_kv, CHUNK))
    const = lambda shape: pl.BlockSpec(shape, lambda b, t, s: (0,) * len(shape))
    grid_spec = pltpu.PrefetchScalarGridSpec(
        num_scalar_prefetch=1,
        grid=(bsz, seq // tile),
        in_specs=[
            pl.BlockSpec((1, tile, d_model), lambda b, t, s: (b, t, 0)),
            const((1, d_model)),
            const((d_model, d_q)), const((1, d_q)),
            const((d_model, d_kv)), const((1, d_kv)),
            const((d_kv, d_model)), const((d_kv, CHUNK)),
            const((d_q, d_model)), const((1, d_model)),
        ],
        out_specs=pl.BlockSpec((1, tile, d_model), lambda b, t, s: (b, t, 0)),
        scratch_shapes=[
            pltpu.VMEM((n_blocks, ATTN_PAIRS * CHUNK, LANES), BF16),
            pltpu.VMEM((4, tile + CHUNK, LANES), BF16),
            pltpu.VMEM((4, n_blocks + 1, LANES, CHUNK), BF16),
            pltpu.VMEM((tile, d_q), BF16),
        ],
    )
    return pl.pallas_call(
        functools.partial(_attn_kernel, tile=tile),
        grid_spec=grid_spec,
        out_shape=jax.ShapeDtypeStruct(h.shape, F32),
        compiler_params=pltpu.CompilerParams(
            dimension_semantics=("arbitrary", "arbitrary"),
            vmem_limit_bytes=VMEM_LIMIT),
        name="attn",
    )(sinks.astype(F32), h, row(g), w_q.astype(BF16), row(b_q), w_k.astype(BF16), row(b_k),
      w_v.T.astype(BF16), b_vt, w_o.astype(BF16), row(b_o))


def kernel(x, norm_mix_g, norm_mlp_g, final_norm_g, w_in_even, w_out_even, gm_ln_g, gm_ln_b,
           gm_w_s, gm_b_s, ssm_conv_w, ssm_conv_b, ssm_dt_bias, ssm_a_log, ssm_d, ssm_norm_g,
           w_qkv, b_qkv, w_o, b_o, attn_sinks, w_up, w_down):
    depth = norm_mix_g.shape[0]
    h = x
    for i in range(depth):
        j = i // 2
        if i % 2 == 0:
            h = _mixer0(h, norm_mix_g[i], w_in_even[j], w_out_even[j], gm_ln_g[j], gm_ln_b[j],
                        gm_w_s[j], gm_b_s[j], ssm_conv_w[j], ssm_conv_b[j], ssm_dt_bias[j],
                        ssm_a_log[j], ssm_d[j], ssm_norm_g[j])
        else:
            h = _attn(h, norm_mix_g[i], w_qkv[j], b_qkv[j], w_o[j], b_o[j], attn_sinks[j])
        h = _mlp(h, norm_mlp_g[i], w_up[i], w_down[i], final_norm_g,
                 final_norm=(i == depth - 1))
    return h
```

```python
import functools

import jax
import jax.numpy as jnp
from jax import lax
from jax.experimental import pallas as pl
from jax.experimental.pallas import tpu as pltpu

F32 = jnp.float32
BF16 = jnp.bfloat16

RMS_EPS = 1e-5
LN_EPS = 1e-5

LANES = 128
CHUNK = 128
HEADDIM = 64
HALF = LANES // 2
LOG2E = 1.4426950408889634

GM_GROUPS = 8
SSM_HEADS = 16
SSM_GROUPS = 4
SSM_STATE = 128
SSM_CONV = 4
SLAB = 512
CONV_CARRY = 8

ATTN_HEADS = 16
ATTN_KV_HEADS = 2
ATTN_PAIRS = ATTN_HEADS // 2
PAIRS_PER_KV = ATTN_PAIRS // ATTN_KV_HEADS

VMEM_LIMIT = 56 * 1024 * 1024


def _dot(a, b):
    return jnp.dot(a, b, preferred_element_type=F32)


def _dot_nt(a, b):
    return lax.dot_general(a, b, (((1,), (1,)), ((), ())), preferred_element_type=F32)


def _dot_exact(a, b):
    return jnp.dot(a, b, preferred_element_type=F32, precision=lax.Precision.HIGHEST)


def _rmsnorm(x, g):
    ms = jnp.mean(x * x, axis=-1, keepdims=True)
    return x * lax.rsqrt(ms + RMS_EPS) * g


def _gelu_tanh(x):
    c = 0.7978845608028654
    half = 0.5 * x
    return half + half * jnp.tanh(x * (c + (0.044715 * c) * (x * x)))


def _silu(x):
    half = 0.5 * x
    return half + half * jnp.tanh(half)


def _softplus(x):
    return jnp.maximum(x, 0.0) + jnp.log1p(jnp.exp(-jnp.abs(x)))


def _mixer0_kernel(x_ref, g_ref, w_in_ref, w_out_ref, lng_ref, lnb_ref, ws_ref, bexp_ref,
                   convw_ref, convb_ref, dtb_ref, alog_ref, dexp_ref, ng_ref, hexp_ref,
                   out_ref, carry_s, state_s, mixa_s, prev_x_s, prev_y_s, prev_z_s,
                   *, tile, tiles_per_seq, d_model, d_inner):
    n_chunks = tile // CHUNK
    d_bc = SSM_GROUPS * SSM_STATE
    step = pl.program_id(0)

    @pl.when(step == 0)
    def _():
        mixa_s[...] = jnp.zeros_like(mixa_s)
        prev_x_s[...] = jnp.zeros_like(prev_x_s)
        prev_y_s[...] = jnp.zeros_like(prev_y_s)
        prev_z_s[...] = jnp.zeros_like(prev_z_s)

    @pl.when(step % tiles_per_seq == 0)
    def _():
        state_s[...] = jnp.zeros_like(state_s)
        carry_s[...] = jnp.zeros_like(carry_s)

    row = lax.broadcasted_iota(jnp.int32, (CHUNK, CHUNK), 0)
    col = lax.broadcasted_iota(jnp.int32, (CHUNK, CHUNK), 1)
    causal = row >= col
    o_u, o_v, o_z, o_x = 0, d_model, 2 * d_model, 2 * d_model + d_inner
    o_dt = o_x + d_inner + 2 * d_bc
    conv_dim = d_inner + 2 * d_bc
    n_slabs_d = d_model // SLAB
    gw = d_inner // SSM_GROUPS


    mixa_prev = mixa_s[...]
    out_acc = [_dot(mixa_prev, w_out_ref[0:d_model, n * SLAB:(n + 1) * SLAB])
               for n in range(n_slabs_d)]
    x = x_ref[0]
    y = _rmsnorm(x, g_ref[...]).astype(BF16)

    def in_proj(lo, width=SLAB):
        return _dot(y, w_in_ref[:, lo:lo + width])

    def conv_silu(raw, lo):
        cols = slice(lo, lo + SLAB)
        head16 = jnp.concatenate([carry_s[:, cols], raw[0:CONV_CARRY, :]], axis=0)
        w_last = convw_ref[SSM_CONV - 1:SSM_CONV, cols]
        conv = convb_ref[:, cols] + w_last * raw
        conv8 = convb_ref[:, cols] + w_last * raw[0:CONV_CARRY, :]
        for k in range(1, SSM_CONV):
            w_k = convw_ref[SSM_CONV - 1 - k:SSM_CONV - k, cols]
            conv = conv + w_k * pltpu.roll(raw, k, axis=0)
            conv8 = conv8 + w_k * pltpu.roll(head16, k, axis=0)[CONV_CARRY:2 * CONV_CARRY, :]
        carry_s[:, cols] = raw[tile - CONV_CARRY:tile, :]
        return _silu(jnp.concatenate([conv8, conv[CONV_CARRY:, :]], axis=0))

    xbc_raw = [in_proj(o_x)]
    yv = prev_y_s[...] * _silu(prev_z_s[...])
    xbc_raw.append(in_proj(o_x + SLAB))
    b_out = jnp.concatenate(
        [_rmsnorm(yv[:, g * gw:(g + 1) * gw], ng_ref[:, g * gw:(g + 1) * gw]).astype(BF16)
         for g in range(SSM_GROUPS)], axis=1)
    xbc_act = []
    for n in range(2, conv_dim // SLAB):
        xbc_raw.append(in_proj(o_x + n * SLAB))
        xbc_act.append(conv_silu(xbc_raw[n - 2], (n - 2) * SLAB))
    dt_raw = in_proj(o_dt, LANES)
    u_raw = [in_proj(o_u)]
    xbc_act.append(conv_silu(xbc_raw[-2], conv_dim - 2 * SLAB))
    u_raw.append(in_proj(o_u + SLAB))
    xbc_act.append(conv_silu(xbc_raw[-1], conv_dim - SLAB))
    xbc = jnp.concatenate(xbc_act, axis=1)

    v_raw = [in_proj(o_v)]
    u = [_gelu_tanh(u_raw[0])]
    dt = _softplus(dt_raw + dtb_ref[...])
    a = dt * (-jnp.exp(alog_ref[...]))
    tril = jnp.where(causal, 1.0, 0.0).astype(F32)
    chunk_terms = []
    for c in range(n_chunks):
        cs = slice(c * CHUNK, (c + 1) * CHUNK)
        dtc = dt[cs, :]
        acum = _dot_exact(tril, a[cs, :])
        last = acum[CHUNK - 1:CHUNK, :]
        te_t = (jnp.exp(last - acum) * dtc).T
        decay8 = _dot_exact(jnp.exp(acum[CHUNK - 8:CHUNK, :]), hexp_ref[...])
        chunk_terms.append((acum, acum.T, dtc.T, te_t, decay8[7:8, :]))
    v_raw.append(in_proj(o_v + SLAB))
    u.append(_gelu_tanh(u_raw[1]))
    u = jnp.concatenate(u, axis=1)

    v = []
    for n in range(n_slabs_d):
        out_acc[n] = out_acc[n] + _dot(b_out, w_out_ref[d_model:, n * SLAB:(n + 1) * SLAB])
        v.append(_gelu_tanh(v_raw[n]))
    out_ref[0] = prev_x_s[...] + jnp.concatenate(out_acc, axis=1)
    v = jnp.concatenate(v, axis=1)

    z_raw = [in_proj(o_z)]
    mu = jnp.mean(v, axis=-1, keepdims=True)
    vc = v - mu
    var = jnp.mean(vc * vc, axis=-1, keepdims=True)
    z_raw.append(in_proj(o_z + SLAB))
    vn = (vc * lax.rsqrt(var + LN_EPS) * lng_ref[...] + lnb_ref[...]).astype(BF16)
    prev_z_s[...] = jnp.concatenate(z_raw, axis=1)
    prev_x_s[...] = x

    xs = xbc[:, 0:d_inner]
    lo_full = lax.broadcasted_iota(jnp.int32, (1, d_inner), 1) % LANES < HALF
    x_lo = jnp.where(lo_full, xs, 0.0).astype(BF16)
    x_hi = jnp.where(lo_full, 0.0, xs).astype(BF16)
    bm = xbc[:, d_inner:d_inner + d_bc]
    cm = xbc[:, d_inner + d_bc:d_inner + 2 * d_bc]
    pairs_per_group = SSM_HEADS // SSM_GROUPS // 2
    state = state_s[...]
    y_chunks = []
    for c in range(n_chunks):
        cs = slice(c * CHUNK, (c + 1) * CHUNK)
        acum, acum_t, dt_t, te_t, chunk_decay = chunk_terms[c]

        st_lo = jnp.where(lo_full, state, 0.0).astype(BF16)
        st_hi = jnp.where(lo_full, 0.0, state).astype(BF16)

        y_pairs = []
        s_pairs = []
        for j in range(SSM_HEADS // 2):
            ps = slice(j * LANES, (j + 1) * LANES)
            if j % pairs_per_group == 0:
                g = j // pairs_per_group
                ns = slice(g * SSM_STATE, (g + 1) * SSM_STATE)
                bg = bm[cs, ns]
                cg = cm[cs, ns]
                cb = _dot_nt(cg.astype(BF16), bg.astype(BF16))
                bg_t = bg.T
            y_acc = None
            s_acc = None
            for h, x_sel, st_sel in ((2 * j, x_lo, st_lo), (2 * j + 1, x_hi, st_hi)):
                a_col = jnp.broadcast_to(acum[:, h:h + 1], (CHUNK, CHUNK))
                seg = a_col - acum_t[h:h + 1, :]
                decay = jnp.exp(jnp.where(causal, seg, -jnp.inf))
                w_intra = (cb * decay * dt_t[h:h + 1, :]).astype(BF16)
                c_in = (cg * jnp.exp(a_col)).astype(BF16)
                xh = x_sel[cs, ps]
                lhs = jnp.concatenate([w_intra, c_in], axis=1)
                rhs = jnp.concatenate([xh, st_sel[:, ps]], axis=0)
                yh = _dot(lhs, rhs)
                sh = _dot((bg_t * te_t[h:h + 1, :]).astype(BF16), xh)
                y_acc = yh if y_acc is None else y_acc + yh
                s_acc = sh if s_acc is None else s_acc + sh
            y_pairs.append(y_acc)
            s_pairs.append(s_acc)
        y_chunks.append(jnp.concatenate(y_pairs, axis=1))
        state = state * chunk_decay + jnp.concatenate(s_pairs, axis=1)
    state_s[...] = state

    prev_y_s[...] = jnp.concatenate(y_chunks, axis=0) + xs * dexp_ref[...]

    for g in range(GM_GROUPS):
        gs = slice(g * CHUNK, (g + 1) * CHUNK)
        w_g = jnp.where(causal, ws_ref[g], 0.0).astype(BF16)
        for c in range(n_chunks):
            cs = slice(c * CHUNK, (c + 1) * CHUNK)
            mixed = _dot(w_g, vn[cs, gs]) + bexp_ref[:, gs]
            mixa_s[cs, gs] = (u[cs, gs] * mixed).astype(BF16)


def _mixer0(h, g, w_in, w_out, ln_g, ln_b, w_s, b_s, conv_w, conv_b, dt_bias, a_log, d_skip,
            norm_g, *, tile=256):
    bsz, seq, d_model = h.shape
    d_inner = d_model
    d_bc = SSM_GROUPS * SSM_STATE
    conv_dim = d_inner + 2 * d_bc
    n_main = 3 * d_model + conv_dim
    assert w_in.shape == (d_model, n_main + SSM_HEADS)
    assert seq % tile == 0 and tile % CHUNK == 0 and d_inner == SSM_HEADS * HEADDIM

    pad = LANES - SSM_HEADS
    w_in_p = jnp.pad(w_in, ((0, 0), (0, pad))).astype(BF16)
    row = lambda v: v.reshape(1, -1).astype(F32)
    pad_row = lambda v: jnp.pad(v.astype(F32), (0, pad)).reshape(1, LANES)
    bexp = jnp.repeat(b_s.T.astype(F32), CHUNK, axis=1)
    dexp = jnp.repeat(d_skip.astype(F32), HEADDIM).reshape(1, d_inner)
    hexp = (jnp.arange(LANES)[:, None] == (jnp.arange(d_inner)[None, :] // HEADDIM)).astype(F32)

    tiles_per_seq = seq // tile
    n_tiles = bsz * tiles_per_seq

    def tile_index(i):
        return i // tiles_per_seq, i % tiles_per_seq, 0

    const = lambda shape: pl.BlockSpec(shape, lambda i: (0,) * len(shape))
    kern = functools.partial(_mixer0_kernel, tile=tile, tiles_per_seq=tiles_per_seq,
                             d_model=d_model, d_inner=d_inner)
    return pl.pallas_call(
        kern,
        grid=(n_tiles + 1,),
        in_specs=[
            pl.BlockSpec((1, tile, d_model), lambda i: tile_index(jnp.minimum(i, n_tiles - 1))),
            const((1, d_model)),
            const(w_in_p.shape),
            const(w_out.shape),
            const((1, d_model)), const((1, d_model)),
            const(w_s.shape),
            const(bexp.shape),
            const(conv_w.shape), const((1, conv_dim)),
            const((1, LANES)), const((1, LANES)),
            const((1, d_inner)), const((1, d_inner)),
            const(hexp.shape),
        ],
        out_specs=pl.BlockSpec((1, tile, d_model), lambda i: tile_index(jnp.maximum(i - 1, 0))),
        out_shape=jax.ShapeDtypeStruct(h.shape, F32),
        scratch_shapes=[
            pltpu.VMEM((CONV_CARRY, conv_dim), F32),
            pltpu.VMEM((SSM_STATE, d_inner), F32),
            pltpu.VMEM((tile, d_model), BF16),
            pltpu.VMEM((tile, d_model), F32),
            pltpu.VMEM((tile, d_inner), F32),
            pltpu.VMEM((tile, d_inner), F32),
        ],
        compiler_params=pltpu.CompilerParams(
            dimension_semantics=("arbitrary",),
            vmem_limit_bytes=VMEM_LIMIT),
        name="mixer0",
    )(h, row(g), w_in_p, w_out.astype(BF16), row(ln_g), row(ln_b), w_s.astype(F32), bexp,
      conv_w.astype(F32), row(conv_b), pad_row(dt_bias), pad_row(a_log), dexp, row(norm_g),
      hexp)


def _mlp_kernel(x_ref, g_ref, w_up_ref, w_down_ref, gf_ref, out_ref, *, ff_chunk, final_norm):
    x = x_ref[...]
    y = _rmsnorm(x, g_ref[...]).astype(BF16)
    acc = x
    d_ff = w_up_ref.shape[1]
    for c in range(d_ff // ff_chunk):
        cs = slice(c * ff_chunk, (c + 1) * ff_chunk)
        hid = jnp.maximum(_dot(y, w_up_ref[:, cs]), 0.0)
        acc = acc + _dot((hid * hid).astype(BF16), w_down_ref[cs, :])
    if final_norm:
        acc = _rmsnorm(acc, gf_ref[...])
    out_ref[...] = acc


def _mlp(h, g, w_up, w_down, g_final, *, final_norm, tile=512, ff_chunk=1024):
    bsz, seq, d_model = h.shape
    n = bsz * seq
    d_ff = w_up.shape[1]
    assert n % tile == 0 and d_ff % ff_chunk == 0
    x2 = h.reshape(n, d_model)
    const = lambda shape: pl.BlockSpec(shape, lambda i: (0,) * len(shape))
    out = pl.pallas_call(
        functools.partial(_mlp_kernel, ff_chunk=ff_chunk, final_norm=final_norm),
        grid=(n // tile,),
        in_specs=[
            pl.BlockSpec((tile, d_model), lambda i: (i, 0)),
            const((1, d_model)),
            const((d_model, d_ff)),
            const((d_ff, d_model)),
            const((1, d_model)),
        ],
        out_specs=pl.BlockSpec((tile, d_model), lambda i: (i, 0)),
        out_shape=jax.ShapeDtypeStruct((n, d_model), F32),
        compiler_params=pltpu.CompilerParams(
            dimension_semantics=("arbitrary",), vmem_limit_bytes=VMEM_LIMIT),
        name="mlp_final" if final_norm else "mlp",
    )(x2, g.reshape(1, -1).astype(F32), w_up.astype(BF16), w_down.astype(BF16),
      g_final.reshape(1, -1).astype(F32))
    return out.reshape(bsz, seq, d_model)


def _attn_kernel(sinks_ref, x_ref, g_ref, wq_ref, bq_ref, wk_ref, bk_ref, wvt_ref, bvt_ref,
                 wo_ref, bo_ref, out_ref, q_s, k_s, vt_s, att_s, *, tile):
    n_blocks = tile // CHUNK
    grp_rows = PAIRS_PER_KV * CHUNK
    t_idx = pl.program_id(1)

    @pl.when(t_idx == 0)
    def _():
        k_s[:, 0:CHUNK, :] = jnp.zeros((4, CHUNK, LANES), BF16)
        vt_s[:, 0] = jnp.zeros((4, LANES, CHUNK), BF16)

    @pl.when(t_idx != 0)
    def _():
        k_s[:, 0:CHUNK, :] = k_s[:, tile:tile + CHUNK, :]
        vt_s[:, 0] = vt_s[:, n_blocks]

    x = x_ref[0]
    y = _rmsnorm(x, g_ref[...]).astype(BF16)
    q = ((_dot(y, wq_ref[...]) + bq_ref[...]) * (HEADDIM ** -0.5 * LOG2E)).astype(BF16)
    for i in range(n_blocks):
        for j in range(ATTN_PAIRS):
            q_s[i, j * CHUNK:(j + 1) * CHUNK, :] = q[i * CHUNK:(i + 1) * CHUNK,
                                                     j * LANES:(j + 1) * LANES]

    k = _dot(y, wk_ref[...]) + bk_ref[...]
    k_swp = pltpu.roll(k, HALF, axis=1)
    lo_lane = lax.broadcasted_iota(jnp.int32, (1, LANES), 1) < HALF
    k_variants = (jnp.where(lo_lane, k, 0.0), jnp.where(lo_lane, 0.0, k_swp),
                  jnp.where(lo_lane, k_swp, 0.0), jnp.where(lo_lane, 0.0, k))
    for g, val in enumerate(k_variants):
        k_s[g, CHUNK:CHUNK + tile, :] = val.astype(BF16)

    bvt = jnp.concatenate([bvt_ref[...]] * n_blocks, axis=1)
    vt = _dot_nt(wvt_ref[...], y) + bvt
    vt_swp = pltpu.roll(vt, HALF, axis=0)
    lo_row = lax.broadcasted_iota(jnp.int32, (LANES, 1), 0) < HALF
    v_variants = (jnp.where(lo_row, vt, 0.0), jnp.where(lo_row, 0.0, vt_swp),
                  jnp.where(lo_row, vt_swp, 0.0), jnp.where(lo_row, 0.0, vt))
    for g, val in enumerate(v_variants):
        for b in range(n_blocks):
            vt_s[g, 1 + b] = val[:, b * CHUNK:(b + 1) * CHUNK].astype(BF16)

    si = lax.broadcasted_iota(jnp.int32, (2 * CHUNK, CHUNK), 0)
    qi = lax.broadcasted_iota(jnp.int32, (2 * CHUNK, CHUNK), 1)
    valid = (si > qi) & (si <= qi + CHUNK)
    neg_inf = jnp.float32(-jnp.inf)
    bias = jnp.where(valid, 0.0, neg_inf)
    bias_first = jnp.where(valid & (si >= CHUNK), 0.0, neg_inf)

    def block_body(i, carry):
        br = pl.ds(pl.multiple_of(i * CHUNK, CHUNK), 2 * CHUNK)
        qr = pl.ds(pl.multiple_of(i * CHUNK, CHUNK), CHUNK)
        first = jnp.logical_and(t_idx == 0, i == 0)
        blk_bias = jnp.where(first, bias_first, bias)

        def scores_t(g):
            kv = g // 2
            return _dot_nt(k_s[g, br, :], q_s[i, kv * grp_rows:(kv + 1) * grp_rows, :])

        st_next = scores_t(0)
        for kv in range(ATTN_KV_HEADS):
            o_acc = None
            for par in range(2):
                g = 2 * kv + par
                st_all = st_next
                if g + 1 < 2 * ATTN_KV_HEADS:
                    st_next = scores_t(g + 1)
                p_cols = []
                inv_cols = []
                for jj in range(PAIRS_PER_KV):
                    head = 2 * (kv * PAIRS_PER_KV + jj) + par
                    s = st_all[:, jj * CHUNK:(jj + 1) * CHUNK] + blk_bias
                    sink = sinks_ref[head] * LOG2E
                    m = jnp.maximum(jnp.max(s, axis=0, keepdims=True), sink)
                    p = jnp.exp2(s - m)
                    denom = jnp.sum(p, axis=0, keepdims=True) + jnp.exp2(sink - m)
                    p_cols.append(p.astype(BF16))
                    inv_cols.append(1.0 / denom)
                v_band = jnp.concatenate([vt_s[g, i], vt_s[g, i + 1]], axis=1)
                ot = _dot(v_band, jnp.concatenate(p_cols, axis=1))
                ot = ot * jnp.concatenate(inv_cols, axis=1)
                o_acc = ot if o_acc is None else o_acc + ot
            for jj in range(PAIRS_PER_KV):
                j = kv * PAIRS_PER_KV + jj
                att_s[qr, j * LANES:(j + 1) * LANES] = (
                    o_acc[:, jj * CHUNK:(jj + 1) * CHUNK].T.astype(BF16))
        return carry

    lax.fori_loop(0, n_blocks, block_body, 0)

    out_ref[0] = x + _dot(att_s[...], wo_ref[...]) + bo_ref[...]


def _attn(h, g, w_qkv, b_qkv, w_o, b_o, sinks, *, tile=512):
    bsz, seq, d_model = h.shape
    d_q = ATTN_HEADS * HEADDIM
    d_kv = ATTN_KV_HEADS * HEADDIM
    assert w_qkv.shape == (d_model, d_q + 2 * d_kv) and d_kv == LANES
    assert seq % tile == 0 and tile % CHUNK == 0
    n_blocks = tile // CHUNK
    row = lambda v: v.reshape(1, -1).astype(F32)
    w_q, w_k, w_v = w_qkv[:, :d_q], w_qkv[:, d_q:d_q + d_kv], w_qkv[:, d_q + d_kv:]
    b_q, b_k, b_v = b_qkv[:d_q], b_qkv[d_q:d_q + d_kv], b_qkv[d_q + d_kv:]
    b_vt = jnp.broadcast_to(b_v.astype(F32)[:, None], (d_kv, CHUNK))
    const = lambda shape: pl.BlockSpec(shape, lambda b, t, s: (0,) * len(shape))
    grid_spec = pltpu.PrefetchScalarGridSpec(
        num_scalar_prefetch=1,
        grid=(bsz, seq // tile),
        in_specs=[
            pl.BlockSpec((1, tile, d_model), lambda b, t, s: (b, t, 0)),
            const((1, d_model)),
            const((d_model, d_q)), const((1, d_q)),
            const((d_model, d_kv)), const((1, d_kv)),
            const((d_kv, d_model)), const((d_kv, CHUNK)),
            const((d_q, d_model)), const((1, d_model)),
        ],
        out_specs=pl.BlockSpec((1, tile, d_model), lambda b, t, s: (b, t, 0)),
        scratch_shapes=[
            pltpu.VMEM((n_blocks, ATTN_PAIRS * CHUNK, LANES), BF16),
            pltpu.VMEM((4, tile + CHUNK, LANES), BF16),
            pltpu.VMEM((4, n_blocks + 1, LANES, CHUNK), BF16),
            pltpu.VMEM((tile, d_q), BF16),
        ],
    )
    return pl.pallas_call(
        functools.partial(_attn_kernel, tile=tile),
        grid_spec=grid_spec,
        out_shape=jax.ShapeDtypeStruct(h.shape, F32),
        compiler_params=pltpu.CompilerParams(
            dimension_semantics=("arbitrary", "arbitrary"),
            vmem_limit_bytes=VMEM_LIMIT),
        name="attn",
    )(sinks.astype(F32), h, row(g), w_q.astype(BF16), row(b_q), w_k.astype(BF16), row(b_k),
      w_v.T.astype(BF16), b_vt, w_o.astype(BF16), row(b_o))


def kernel(x, norm_mix_g, norm_mlp_g, final_norm_g, w_in_even, w_out_even, gm_ln_g, gm_ln_b,
           gm_w_s, gm_b_s, ssm_conv_w, ssm_conv_b, ssm_dt_bias, ssm_a_log, ssm_d, ssm_norm_g,
           w_qkv, b_qkv, w_o, b_o, attn_sinks, w_up, w_down):
    depth = norm_mix_g.shape[0]
    h = x
    for i in range(depth):
        j = i // 2
        if i % 2 == 0:
            h = _mixer0(h, norm_mix_g[i], w_in_even[j], w_out_even[j], gm_ln_g[j], gm_ln_b[j],
                        gm_w_s[j], gm_b_s[j], ssm_conv_w[j], ssm_conv_b[j], ssm_dt_bias[j],
                        ssm_a_log[j], ssm_d[j], ssm_norm_g[j])
        else:
            h = _attn(h, norm_mix_g[i], w_qkv[j], b_qkv[j], w_o[j], b_o[j], attn_sinks[j])
        h = _mlp(h, norm_mlp_g[i], w_up[i], w_down[i], final_norm_g,
                 final_norm=(i == depth - 1))
    return h
```

```python
import functools

import jax
import jax.numpy as jnp
from jax import lax
from jax.experimental import pallas as pl
from jax.experimental.pallas import tpu as pltpu

F32 = jnp.float32
BF16 = jnp.bfloat16

RMS_EPS = 1e-5
LN_EPS = 1e-5

LANES = 128
CHUNK = 128
HEADDIM = 64
HALF = LANES // 2
LOG2E = 1.4426950408889634

GM_GROUPS = 8
SSM_HEADS = 16
SSM_GROUPS = 4
SSM_STATE = 128
SSM_CONV = 4
SLAB = 512
CONV_CARRY = 8

ATTN_HEADS = 16
ATTN_KV_HEADS = 2
ATTN_PAIRS = ATTN_HEADS // 2
PAIRS_PER_KV = ATTN_PAIRS // ATTN_KV_HEADS

VMEM_LIMIT = 56 * 1024 * 1024


def _dot(a, b):
    return jnp.dot(a, b, preferred_element_type=F32)


def _dot_nt(a, b):
    return lax.dot_general(a, b, (((1,), (1,)), ((), ())), preferred_element_type=F32)


def _dot_exact(a, b):
    return jnp.dot(a, b, preferred_element_type=F32, precision=lax.Precision.HIGHEST)


def _rmsnorm(x, g):
    ms = jnp.mean(x * x, axis=-1, keepdims=True)
    return x * lax.rsqrt(ms + RMS_EPS) * g


def _gelu_tanh(x):
    c = 0.7978845608028654
    half = 0.5 * x
    return half + half * jnp.tanh(x * (c + (0.044715 * c) * (x * x)))


def _silu(x):
    half = 0.5 * x
    return half + half * jnp.tanh(half)


def _softplus(x):
    return jnp.maximum(x, 0.0) + jnp.log1p(jnp.exp(-jnp.abs(x)))


def _mixer0_kernel(x_ref, g_ref, w_in_ref, w_out_ref, lng_ref, lnb_ref, ws_ref, bexp_ref,
                   convw_ref, convb_ref, dtb_ref, alog_ref, dexp_ref, ng_ref, hexp_ref,
                   out_ref, carry_s, state_s, mixa_s, prev_x_s, prev_y_s, prev_z_s,
                   *, tile, tiles_per_seq, d_model, d_inner):
    n_chunks = tile // CHUNK
    d_bc = SSM_GROUPS * SSM_STATE
    step = pl.program_id(0)

    @pl.when(step == 0)
    def _():
        mixa_s[...] = jnp.zeros_like(mixa_s)
        prev_x_s[...] = jnp.zeros_like(prev_x_s)
        prev_y_s[...] = jnp.zeros_like(prev_y_s)
        prev_z_s[...] = jnp.zeros_like(prev_z_s)

    @pl.when(step % tiles_per_seq == 0)
    def _():
        state_s[...] = jnp.zeros_like(state_s)
        carry_s[...] = jnp.zeros_like(carry_s)

    row = lax.broadcasted_iota(jnp.int32, (CHUNK, CHUNK), 0)
    col = lax.broadcasted_iota(jnp.int32, (CHUNK, CHUNK), 1)
    causal = row >= col
    o_u, o_v, o_z, o_x = 0, d_model, 2 * d_model, 2 * d_model + d_inner
    o_dt = o_x + d_inner + 2 * d_bc
    conv_dim = d_inner + 2 * d_bc
    n_slabs_d = d_model // SLAB
    gw = d_inner // SSM_GROUPS


    mixa_prev = mixa_s[...]
    out_acc = [_dot(mixa_prev, w_out_ref[0:d_model, n * SLAB:(n + 1) * SLAB])
               for n in range(n_slabs_d)]
    x = x_ref[0]
    y = _rmsnorm(x, g_ref[...]).astype(BF16)

    def in_proj(lo, width=SLAB):
        return _dot(y, w_in_ref[:, lo:lo + width])

    def conv_silu(raw, lo):
        cols = slice(lo, lo + SLAB)
        head16 = jnp.concatenate([carry_s[:, cols], raw[0:CONV_CARRY, :]], axis=0)
        w_last = convw_ref[SSM_CONV - 1:SSM_CONV, cols]
        conv = convb_ref[:, cols] + w_last * raw
        conv8 = convb_ref[:, cols] + w_last * raw[0:CONV_CARRY, :]
        for k in range(1, SSM_CONV):
            w_k = convw_ref[SSM_CONV - 1 - k:SSM_CONV - k, cols]
            conv = conv + w_k * pltpu.roll(raw, k, axis=0)
            conv8 = conv8 + w_k * pltpu.roll(head16, k, axis=0)[CONV_CARRY:2 * CONV_CARRY, :]
        carry_s[:, cols] = raw[tile - CONV_CARRY:tile, :]
        return _silu(jnp.concatenate([conv8, conv[CONV_CARRY:, :]], axis=0))

    xbc_raw = [in_proj(o_x)]
    yv = prev_y_s[...] * _silu(prev_z_s[...])
    xbc_raw.append(in_proj(o_x + SLAB))
    b_out = jnp.concatenate(
        [_rmsnorm(yv[:, g * gw:(g + 1) * gw], ng_ref[:, g * gw:(g + 1) * gw]).astype(BF16)
         for g in range(SSM_GROUPS)], axis=1)
    xbc_act = []
    for n in range(2, conv_dim // SLAB):
        xbc_raw.append(in_proj(o_x + n * SLAB))
        xbc_act.append(conv_silu(xbc_raw[n - 2], (n - 2) * SLAB))
    dt_raw = in_proj(o_dt, LANES)
    u_raw = [in_proj(o_u)]
    xbc_act.append(conv_silu(xbc_raw[-2], conv_dim - 2 * SLAB))
    u_raw.append(in_proj(o_u + SLAB))
    xbc_act.append(conv_silu(xbc_raw[-1], conv_dim - SLAB))
    xbc = jnp.concatenate(xbc_act, axis=1)

    v_raw = [in_proj(o_v)]
    u = [_gelu_tanh(u_raw[0])]
    dt = _softplus(dt_raw + dtb_ref[...])
    a = dt * (-jnp.exp(alog_ref[...]))
    v_raw.append(in_proj(o_v + SLAB))
    u.append(_gelu_tanh(u_raw[1]))
    u = jnp.concatenate(u, axis=1)

    tril = jnp.where(causal, 1.0, 0.0).astype(F32)
    acums = [_dot_exact(tril, a[c * CHUNK:(c + 1) * CHUNK, :]) for c in range(n_chunks)]

    v = []
    for n in range(n_slabs_d):
        out_acc[n] = out_acc[n] + _dot(b_out, w_out_ref[d_model:, n * SLAB:(n + 1) * SLAB])
        v.append(_gelu_tanh(v_raw[n]))
    out_ref[0] = prev_x_s[...] + jnp.concatenate(out_acc, axis=1)
    v = jnp.concatenate(v, axis=1)

    chunk_terms = []
    for c in range(n_chunks):
        acum = acums[c]
        dtc = dt[c * CHUNK:(c + 1) * CHUNK, :]
        last = acum[CHUNK - 1:CHUNK, :]
        te_t = (jnp.exp(last - acum) * dtc).T
        decay8 = _dot_exact(jnp.exp(acum[CHUNK - 8:CHUNK, :]), hexp_ref[...])
        chunk_terms.append((acum, acum.T, dtc.T, te_t, decay8[7:8, :]))

    z_raw = [in_proj(o_z)]
    mu = jnp.mean(v, axis=-1, keepdims=True)
    vc = v - mu
    var = jnp.mean(vc * vc, axis=-1, keepdims=True)
    z_raw.append(in_proj(o_z + SLAB))
    vn = (vc * lax.rsqrt(var + LN_EPS) * lng_ref[...] + lnb_ref[...]).astype(BF16)
    prev_z_s[...] = jnp.concatenate(z_raw, axis=1)
    prev_x_s[...] = x

    xs = xbc[:, 0:d_inner]
    lo_full = lax.broadcasted_iota(jnp.int32, (1, d_inner), 1) % LANES < HALF
    x_lo = jnp.where(lo_full, xs, 0.0).astype(BF16)
    x_hi = jnp.where(lo_full, 0.0, xs).astype(BF16)
    bm = xbc[:, d_inner:d_inner + d_bc]
    cm = xbc[:, d_inner + d_bc:d_inner + 2 * d_bc]
    bm_bf = bm.astype(BF16)
    cm_bf = cm.astype(BF16)
    pairs_per_group = SSM_HEADS // SSM_GROUPS // 2
    state = state_s[...]
    y_chunks = []
    for c in range(n_chunks):
        cs = slice(c * CHUNK, (c + 1) * CHUNK)
        acum, acum_t, dt_t, te_t, chunk_decay = chunk_terms[c]

        st_lo = jnp.where(lo_full, state, 0.0).astype(BF16)
        st_hi = jnp.where(lo_full, 0.0, state).astype(BF16)

        y_pairs = []
        s_pairs = []
        for j in range(SSM_HEADS // 2):
            ps = slice(j * LANES, (j + 1) * LANES)
            if j % pairs_per_group == 0:
                g = j // pairs_per_group
                ns = slice(g * SSM_STATE, (g + 1) * SSM_STATE)
                bg = bm[cs, ns]
                cg = cm[cs, ns]
                cb = _dot_nt(cm_bf[cs, ns], bm_bf[cs, ns])
                bg_t = bg.T
            y_acc = None
            s_acc = None
            for h, x_sel, st_sel in ((2 * j, x_lo, st_lo), (2 * j + 1, x_hi, st_hi)):
                a_col = jnp.broadcast_to(acum[:, h:h + 1], (CHUNK, CHUNK))
                seg = a_col - acum_t[h:h + 1, :]
                decay = jnp.exp(jnp.where(causal, seg, -jnp.inf))
                w_intra = (cb * decay * dt_t[h:h + 1, :]).astype(BF16)
                c_in = (cg * jnp.exp(a_col)).astype(BF16)
                xh = x_sel[cs, ps]
                lhs = jnp.concatenate([w_intra, c_in], axis=1)
                rhs = jnp.concatenate([xh, st_sel[:, ps]], axis=0)
                yh = _dot(lhs, rhs)
                sh = _dot((bg_t * te_t[h:h + 1, :]).astype(BF16), xh)
                y_acc = yh if y_acc is None else y_acc + yh
                s_acc = sh if s_acc is None else s_acc + sh
            y_pairs.append(y_acc)
            s_pairs.append(s_acc)
        y_chunks.append(jnp.concatenate(y_pairs, axis=1))
        state = state * chunk_decay + jnp.concatenate(s_pairs, axis=1)
    state_s[...] = state

    prev_y_s[...] = jnp.concatenate(y_chunks, axis=0) + xs * dexp_ref[...]

    for g in range(GM_GROUPS):
        gs = slice(g * CHUNK, (g + 1) * CHUNK)
        w_g = jnp.where(causal, ws_ref[g], 0.0).astype(BF16)
        for c in range(n_chunks):
            cs = slice(c * CHUNK, (c + 1) * CHUNK)
            mixed = _dot(w_g, vn[cs, gs]) + bexp_ref[:, gs]
            mixa_s[cs, gs] = (u[cs, gs] * mixed).astype(BF16)


def _mixer0(h, g, w_in, w_out, ln_g, ln_b, w_s, b_s, conv_w, conv_b, dt_bias, a_log, d_skip,
            norm_g, *, tile=256):
    bsz, seq, d_model = h.shape
    d_inner = d_model
    d_bc = SSM_GROUPS * SSM_STATE
    conv_dim = d_inner + 2 * d_bc
    n_main = 3 * d_model + conv_dim
    assert w_in.shape == (d_model, n_main + SSM_HEADS)
    assert seq % tile == 0 and tile % CHUNK == 0 and d_inner == SSM_HEADS * HEADDIM

    pad = LANES - SSM_HEADS
    w_in_p = jnp.pad(w_in, ((0, 0), (0, pad))).astype(BF16)
    row = lambda v: v.reshape(1, -1).astype(F32)
    pad_row = lambda v: jnp.pad(v.astype(F32), (0, pad)).reshape(1, LANES)
    bexp = jnp.repeat(b_s.T.astype(F32), CHUNK, axis=1)
    dexp = jnp.repeat(d_skip.astype(F32), HEADDIM).reshape(1, d_inner)
    hexp = (jnp.arange(LANES)[:, None] == (jnp.arange(d_inner)[None, :] // HEADDIM)).astype(F32)

    tiles_per_seq = seq // tile
    n_tiles = bsz * tiles_per_seq

    def tile_index(i):
        return i // tiles_per_seq, i % tiles_per_seq, 0

    const = lambda shape: pl.BlockSpec(shape, lambda i: (0,) * len(shape))
    kern = functools.partial(_mixer0_kernel, tile=tile, tiles_per_seq=tiles_per_seq,
                             d_model=d_model, d_inner=d_inner)
    return pl.pallas_call(
        kern,
        grid=(n_tiles + 1,),
        in_specs=[
            pl.BlockSpec((1, tile, d_model), lambda i: tile_index(jnp.minimum(i, n_tiles - 1))),
            const((1, d_model)),
            const(w_in_p.shape),
            const(w_out.shape),
            const((1, d_model)), const((1, d_model)),
            const(w_s.shape),
            const(bexp.shape),
            const(conv_w.shape), const((1, conv_dim)),
            const((1, LANES)), const((1, LANES)),
            const((1, d_inner)), const((1, d_inner)),
            const(hexp.shape),
        ],
        out_specs=pl.BlockSpec((1, tile, d_model), lambda i: tile_index(jnp.maximum(i - 1, 0))),
        out_shape=jax.ShapeDtypeStruct(h.shape, F32),
        scratch_shapes=[
            pltpu.VMEM((CONV_CARRY, conv_dim), F32),
            pltpu.VMEM((SSM_STATE, d_inner), F32),
            pltpu.VMEM((tile, d_model), BF16),
            pltpu.VMEM((tile, d_model), F32),
            pltpu.VMEM((tile, d_inner), F32),
            pltpu.VMEM((tile, d_inner), F32),
        ],
        compiler_params=pltpu.CompilerParams(
            dimension_semantics=("arbitrary",),
            vmem_limit_bytes=VMEM_LIMIT),
        name="mixer0",
    )(h, row(g), w_in_p, w_out.astype(BF16), row(ln_g), row(ln_b), w_s.astype(F32), bexp,
      conv_w.astype(F32), row(conv_b), pad_row(dt_bias), pad_row(a_log), dexp, row(norm_g),
      hexp)


def _mlp_kernel(x_ref, g_ref, w_up_ref, w_down_ref, gf_ref, out_ref, *, ff_chunk, final_norm):
    x = x_ref[...]
    y = _rmsnorm(x, g_ref[...]).astype(BF16)
    acc = x
    d_ff = w_up_ref.shape[1]
    for c in range(d_ff // ff_chunk):
        cs = slice(c * ff_chunk, (c + 1) * ff_chunk)
        hid = jnp.maximum(_dot(y, w_up_ref[:, cs]), 0.0)
        acc = acc + _dot((hid * hid).astype(BF16), w_down_ref[cs, :])
    if final_norm:
        acc = _rmsnorm(acc, gf_ref[...])
    out_ref[...] = acc


def _mlp(h, g, w_up, w_down, g_final, *, final_norm, tile=512, ff_chunk=1024):
    bsz, seq, d_model = h.shape
    n = bsz * seq
    d_ff = w_up.shape[1]
    assert n % tile == 0 and d_ff % ff_chunk == 0
    x2 = h.reshape(n, d_model)
    const = lambda shape: pl.BlockSpec(shape, lambda i: (0,) * len(shape))
    out = pl.pallas_call(
        functools.partial(_mlp_kernel, ff_chunk=ff_chunk, final_norm=final_norm),
        grid=(n // tile,),
        in_specs=[
            pl.BlockSpec((tile, d_model), lambda i: (i, 0)),
            const((1, d_model)),
            const((d_model, d_ff)),
            const((d_ff, d_model)),
            const((1, d_model)),
        ],
        out_specs=pl.BlockSpec((tile, d_model), lambda i: (i, 0)),
        out_shape=jax.ShapeDtypeStruct((n, d_model), F32),
        compiler_params=pltpu.CompilerParams(
            dimension_semantics=("arbitrary",), vmem_limit_bytes=VMEM_LIMIT),
        name="mlp_final" if final_norm else "mlp",
    )(x2, g.reshape(1, -1).astype(F32), w_up.astype(BF16), w_down.astype(BF16),
      g_final.reshape(1, -1).astype(F32))
    return out.reshape(bsz, seq, d_model)


def _attn_kernel(sinks_ref, x_ref, g_ref, wq_ref, bq_ref, wk_ref, bk_ref, wvt_ref, bvt_ref,
                 wo_ref, bo_ref, out_ref, q_s, k_s, vt_s, att_s, *, tile):
    n_blocks = tile // CHUNK
    grp_rows = PAIRS_PER_KV * CHUNK
    t_idx = pl.program_id(1)

    @pl.when(t_idx == 0)
    def _():
        k_s[:, 0:CHUNK, :] = jnp.zeros((4, CHUNK, LANES), BF16)
        vt_s[:, 0] = jnp.zeros((4, LANES, CHUNK), BF16)

    @pl.when(t_idx != 0)
    def _():
        k_s[:, 0:CHUNK, :] = k_s[:, tile:tile + CHUNK, :]
        vt_s[:, 0] = vt_s[:, n_blocks]

    x = x_ref[0]
    y = _rmsnorm(x, g_ref[...]).astype(BF16)
    q = ((_dot(y, wq_ref[...]) + bq_ref[...]) * (HEADDIM ** -0.5 * LOG2E)).astype(BF16)
    for i in range(n_blocks):
        for j in range(ATTN_PAIRS):
            q_s[i, j * CHUNK:(j + 1) * CHUNK, :] = q[i * CHUNK:(i + 1) * CHUNK,
                                                     j * LANES:(j + 1) * LANES]

    k = _dot(y, wk_ref[...]) + bk_ref[...]
    k_swp = pltpu.roll(k, HALF, axis=1)
    lo_lane = lax.broadcasted_iota(jnp.int32, (1, LANES), 1) < HALF
    k_variants = (jnp.where(lo_lane, k, 0.0), jnp.where(lo_lane, 0.0, k_swp),
                  jnp.where(lo_lane, k_swp, 0.0), jnp.where(lo_lane, 0.0, k))
    for g, val in enumerate(k_variants):
        k_s[g, CHUNK:CHUNK + tile, :] = val.astype(BF16)

    bvt = jnp.concatenate([bvt_ref[...]] * n_blocks, axis=1)
    vt = _dot_nt(wvt_ref[...], y) + bvt
    vt_swp = pltpu.roll(vt, HALF, axis=0)
    lo_row = lax.broadcasted_iota(jnp.int32, (LANES, 1), 0) < HALF
    v_variants = (jnp.where(lo_row, vt, 0.0), jnp.where(lo_row, 0.0, vt_swp),
                  jnp.where(lo_row, vt_swp, 0.0), jnp.where(lo_row, 0.0, vt))
    for g, val in enumerate(v_variants):
        for b in range(n_blocks):
            vt_s[g, 1 + b] = val[:, b * CHUNK:(b + 1) * CHUNK].astype(BF16)

    si = lax.broadcasted_iota(jnp.int32, (2 * CHUNK, CHUNK), 0)
    qi = lax.broadcasted_iota(jnp.int32, (2 * CHUNK, CHUNK), 1)
    valid = (si > qi) & (si <= qi + CHUNK)
    neg_inf = jnp.float32(-jnp.inf)
    bias = jnp.where(valid, 0.0, neg_inf)
    bias_first = jnp.where(valid & (si >= CHUNK), 0.0, neg_inf)

    n_var = 2 * ATTN_KV_HEADS

    def scores_t(i, g):
        kv = g // 2
        return _dot_nt(k_s[g, i * CHUNK:(i + 2) * CHUNK, :],
                       q_s[i, kv * grp_rows:(kv + 1) * grp_rows, :])

    st_next = scores_t(0, 0)
    for i in range(n_blocks):
        blk_bias = jnp.where(t_idx == 0, bias_first, bias) if i == 0 else bias
        for kv in range(ATTN_KV_HEADS):
            o_acc = None
            for par in range(2):
                g = 2 * kv + par
                st_all = st_next
                if g + 1 < n_var:
                    st_next = scores_t(i, g + 1)
                elif i + 1 < n_blocks:
                    st_next = scores_t(i + 1, 0)
                p_cols = []
                inv_cols = []
                for jj in range(PAIRS_PER_KV):
                    head = 2 * (kv * PAIRS_PER_KV + jj) + par
                    s = st_all[:, jj * CHUNK:(jj + 1) * CHUNK] + blk_bias
                    sink = sinks_ref[head] * LOG2E
                    m = jnp.maximum(jnp.max(s, axis=0, keepdims=True), sink)
                    p = jnp.exp2(s - m)
                    denom = jnp.sum(p, axis=0, keepdims=True) + jnp.exp2(sink - m)
                    p_cols.append(p.astype(BF16))
                    inv_cols.append(1.0 / denom)
                v_band = jnp.concatenate([vt_s[g, i], vt_s[g, i + 1]], axis=1)
                ot = _dot(v_band, jnp.concatenate(p_cols, axis=1))
                ot = ot * jnp.concatenate(inv_cols, axis=1)
                o_acc = ot if o_acc is None else o_acc + ot
            for jj in range(PAIRS_PER_KV):
                j = kv * PAIRS_PER_KV + jj
                att_s[i * CHUNK:(i + 1) * CHUNK, j * LANES:(j + 1) * LANES] = (
                    o_acc[:, jj * CHUNK:(jj + 1) * CHUNK].T.astype(BF16))

    out_ref[0] = x + _dot(att_s[...], wo_ref[...]) + bo_ref[...]


def _attn(h, g, w_qkv, b_qkv, w_o, b_o, sinks, *, tile=512):
    bsz, seq, d_model = h.shape
    d_q = ATTN_HEADS * HEADDIM
    d_kv = ATTN_KV_HEADS * HEADDIM
    assert w_qkv.shape == (d_model, d_q + 2 * d_kv) and d_kv == LANES
    assert seq % tile == 0 and tile % CHUNK == 0
    n_blocks = tile // CHUNK
    row = lambda v: v.reshape(1, -1).astype(F32)
    w_q, w_k, w_v = w_qkv[:, :d_q], w_qkv[:, d_q:d_q + d_kv], w_qkv[:, d_q + d_kv:]
    b_q, b_k, b_v = b_qkv[:d_q], b_qkv[d_q:d_q + d_kv], b_qkv[d_q + d_kv:]
    b_vt = jnp.broadcast_to(b_v.astype(F32)[:, None], (d_kv, CHUNK))
    const = lambda shape: pl.BlockSpec(shape, lambda b, t, s: (0,) * len(shape))
    grid_spec = pltpu.PrefetchScalarGridSpec(
        num_scalar_prefetch=1,
        grid=(bsz, seq // tile),
        in_specs=[
            pl.BlockSpec((1, tile, d_model), lambda b, t, s: (b, t, 0)),
            const((1, d_model)),
            const((d_model, d_q)), const((1, d_q)),
            const((d_model, d_kv)), const((1, d_kv)),
            const((d_kv, d_model)), const((d_kv, CHUNK)),
            const((d_q, d_model)), const((1, d_model)),
        ],
        out_specs=pl.BlockSpec((1, tile, d_model), lambda b, t, s: (b, t, 0)),
        scratch_shapes=[
            pltpu.VMEM((n_blocks, ATTN_PAIRS * CHUNK, LANES), BF16),
            pltpu.VMEM((4, tile + CHUNK, LANES), BF16),
            pltpu.VMEM((4, n_blocks + 1, LANES, CHUNK), BF16),
            pltpu.VMEM((tile, d_q), BF16),
        ],
    )
    return pl.pallas_call(
        functools.partial(_attn_kernel, tile=tile),
        grid_spec=grid_spec,
        out_shape=jax.ShapeDtypeStruct(h.shape, F32),
        compiler_params=pltpu.CompilerParams(
            dimension_semantics=("arbitrary", "arbitrary"),
            vmem_limit_bytes=VMEM_LIMIT),
        name="attn",
    )(sinks.astype(F32), h, row(g), w_q.astype(BF16), row(b_q), w_k.astype(BF16), row(b_k),
      w_v.T.astype(BF16), b_vt, w_o.astype(BF16), row(b_o))


def kernel(x, norm_mix_g, norm_mlp_g, final_norm_g, w_in_even, w_out_even, gm_ln_g, gm_ln_b,
           gm_w_s, gm_b_s, ssm_conv_w, ssm_conv_b, ssm_dt_bias, ssm_a_log, ssm_d, ssm_norm_g,
           w_qkv, b_qkv, w_o, b_o, attn_sinks, w_up, w_down):
    depth = norm_mix_g.shape[0]
    h = x
    for i in range(depth):
        j = i // 2
        if i % 2 == 0:
            h = _mixer0(h, norm_mix_g[i], w_in_even[j], w_out_even[j], gm_ln_g[j], gm_ln_b[j],
                        gm_w_s[j], gm_b_s[j], ssm_conv_w[j], ssm_conv_b[j], ssm_dt_bias[j],
                        ssm_a_log[j], ssm_d[j], ssm_norm_g[j])
        else:
            h = _attn(h, norm_mix_g[i], w_qkv[j], b_qkv[j], w_o[j], b_o[j], attn_sinks[j])
        h = _mlp(h, norm_mlp_g[i], w_up[i], w_down[i], final_norm_g,
                 final_norm=(i == depth - 1))
    return h
```

```python
import functools

import jax
import jax.numpy as jnp
from jax import lax
from jax.experimental import pallas as pl
from jax.experimental.pallas import tpu as pltpu

F32 = jnp.float32
BF16 = jnp.bfloat16

RMS_EPS = 1e-5
LN_EPS = 1e-5

LANES = 128
CHUNK = 128
HEADDIM = 64
HALF = LANES // 2
LOG2E = 1.4426950408889634

GM_GROUPS = 8
SSM_HEADS = 16
SSM_GROUPS = 4
SSM_STATE = 128
SSM_CONV = 4
SLAB = 512
CONV_CARRY = 8

ATTN_HEADS = 16
ATTN_KV_HEADS = 2
ATTN_PAIRS = ATTN_HEADS // 2
PAIRS_PER_KV = ATTN_PAIRS // ATTN_KV_HEADS

VMEM_LIMIT = 56 * 1024 * 1024


def _dot(a, b):
    return jnp.dot(a, b, preferred_element_type=F32)


def _dot_nt(a, b):
    return lax.dot_general(a, b, (((1,), (1,)), ((), ())), preferred_element_type=F32)


def _split3(x):
    hi = x.astype(BF16)
    r1 = x - hi.astype(F32)
    mid = r1.astype(BF16)
    lo = (r1 - mid.astype(F32)).astype(BF16)
    return hi, mid, lo


def _dot_sel_lhs(sel, b):
    sel = sel.astype(BF16)
    hi, mid, lo = _split3(b)
    return _dot(sel, hi) + _dot(sel, mid) + _dot(sel, lo)


def _dot_sel_rhs(a, sel):
    sel = sel.astype(BF16)
    hi, mid, lo = _split3(a)
    return _dot(hi, sel) + _dot(mid, sel) + _dot(lo, sel)


def _rmsnorm(x, g):
    ms = jnp.mean(x * x, axis=-1, keepdims=True)
    return x * lax.rsqrt(ms + RMS_EPS) * g


def _gelu_tanh(x):
    c = 0.7978845608028654
    half = 0.5 * x
    return half + half * jnp.tanh(x * (c + (0.044715 * c) * (x * x)))


def _silu(x):
    half = 0.5 * x
    return half + half * jnp.tanh(half)


def _softplus(x):
    return jnp.maximum(x, 0.0) + jnp.log1p(jnp.exp(-jnp.abs(x)))


def _mixer0_kernel(x_ref, g_ref, w_in_ref, w_out_ref, lng_ref, lnb_ref, ws_ref, bexp_ref,
                   convw_ref, convb_ref, dtb_ref, alog_ref, dexp_ref, ng_ref, hexp_ref,
                   out_ref, carry_s, state_s, mixa_s, prev_x_s, prev_y_s, prev_z_s,
                   *, tile, tiles_per_seq, d_model, d_inner):
    n_chunks = tile // CHUNK
    d_bc = SSM_GROUPS * SSM_STATE
    step = pl.program_id(0)

    @pl.when(step == 0)
    def _():
        mixa_s[...] = jnp.zeros_like(mixa_s)
        prev_x_s[...] = jnp.zeros_like(prev_x_s)
        prev_y_s[...] = jnp.zeros_like(prev_y_s)
        prev_z_s[...] = jnp.zeros_like(prev_z_s)

    @pl.when(step % tiles_per_seq == 0)
    def _():
        state_s[...] = jnp.zeros_like(state_s)
        carry_s[...] = jnp.zeros_like(carry_s)

    row = lax.broadcasted_iota(jnp.int32, (CHUNK, CHUNK), 0)
    col = lax.broadcasted_iota(jnp.int32, (CHUNK, CHUNK), 1)
    causal = row >= col
    o_u, o_v, o_z, o_x = 0, d_model, 2 * d_model, 2 * d_model + d_inner
    o_dt = o_x + d_inner + 2 * d_bc
    conv_dim = d_inner + 2 * d_bc
    n_slabs_d = d_model // SLAB
    gw = d_inner // SSM_GROUPS


    mixa_prev = mixa_s[...]
    out_acc = [_dot(mixa_prev, w_out_ref[0:d_model, n * SLAB:(n + 1) * SLAB])
               for n in range(n_slabs_d)]
    x = x_ref[0]
    y = _rmsnorm(x, g_ref[...]).astype(BF16)

    def in_proj(lo, width=SLAB):
        return _dot(y, w_in_ref[:, lo:lo + width])

    def conv_silu(raw, lo):
        cols = slice(lo, lo + SLAB)
        head16 = jnp.concatenate([carry_s[:, cols], raw[0:CONV_CARRY, :]], axis=0)
        w_last = convw_ref[SSM_CONV - 1:SSM_CONV, cols]
        conv = convb_ref[:, cols] + w_last * raw
        conv8 = convb_ref[:, cols] + w_last * raw[0:CONV_CARRY, :]
        for k in range(1, SSM_CONV):
            w_k = convw_ref[SSM_CONV - 1 - k:SSM_CONV - k, cols]
            conv = conv + w_k * pltpu.roll(raw, k, axis=0)
            conv8 = conv8 + w_k * pltpu.roll(head16, k, axis=0)[CONV_CARRY:2 * CONV_CARRY, :]
        carry_s[:, cols] = raw[tile - CONV_CARRY:tile, :]
        return _silu(jnp.concatenate([conv8, conv[CONV_CARRY:, :]], axis=0))

    xbc_raw = [in_proj(o_x)]
    yv = prev_y_s[...] * _silu(prev_z_s[...])
    xbc_raw.append(in_proj(o_x + SLAB))
    b_out = jnp.concatenate(
        [_rmsnorm(yv[:, g * gw:(g + 1) * gw], ng_ref[:, g * gw:(g + 1) * gw]).astype(BF16)
         for g in range(SSM_GROUPS)], axis=1)
    xbc_act = []
    for n in range(2, conv_dim // SLAB):
        xbc_raw.append(in_proj(o_x + n * SLAB))
        xbc_act.append(conv_silu(xbc_raw[n - 2], (n - 2) * SLAB))
    dt_raw = in_proj(o_dt, LANES)
    u_raw = [in_proj(o_u)]
    xbc_act.append(conv_silu(xbc_raw[-2], conv_dim - 2 * SLAB))
    u_raw.append(in_proj(o_u + SLAB))
    xbc_act.append(conv_silu(xbc_raw[-1], conv_dim - SLAB))
    xbc = jnp.concatenate(xbc_act, axis=1)

    v_raw = [in_proj(o_v)]
    u = [_gelu_tanh(u_raw[0])]
    dt = _softplus(dt_raw + dtb_ref[...])
    a = dt * (-jnp.exp(alog_ref[...]))
    v_raw.append(in_proj(o_v + SLAB))
    u.append(_gelu_tanh(u_raw[1]))
    u = jnp.concatenate(u, axis=1)

    tril = jnp.where(causal, 1.0, 0.0).astype(F32)
    acums = [_dot_sel_lhs(tril, a[c * CHUNK:(c + 1) * CHUNK, :]) for c in range(n_chunks)]

    v = []
    for n in range(n_slabs_d):
        out_acc[n] = out_acc[n] + _dot(b_out, w_out_ref[d_model:, n * SLAB:(n + 1) * SLAB])
        v.append(_gelu_tanh(v_raw[n]))
    out_ref[0] = prev_x_s[...] + jnp.concatenate(out_acc, axis=1)
    v = jnp.concatenate(v, axis=1)

    chunk_terms = []
    for c in range(n_chunks):
        acum = acums[c]
        dtc = dt[c * CHUNK:(c + 1) * CHUNK, :]
        last = acum[CHUNK - 1:CHUNK, :]
        te_t = (jnp.exp(last - acum) * dtc).T
        decay8 = _dot_sel_rhs(jnp.exp(acum[CHUNK - 8:CHUNK, :]), hexp_ref[...])
        chunk_terms.append((acum, acum.T, dtc.T, te_t, decay8[7:8, :]))

    z_raw = [in_proj(o_z)]
    mu = jnp.mean(v, axis=-1, keepdims=True)
    vc = v - mu
    var = jnp.mean(vc * vc, axis=-1, keepdims=True)
    z_raw.append(in_proj(o_z + SLAB))
    vn = (vc * lax.rsqrt(var + LN_EPS) * lng_ref[...] + lnb_ref[...]).astype(BF16)
    prev_z_s[...] = jnp.concatenate(z_raw, axis=1)
    prev_x_s[...] = x

    xs = xbc[:, 0:d_inner]
    lo_full = lax.broadcasted_iota(jnp.int32, (1, d_inner), 1) % LANES < HALF
    x_lo = jnp.where(lo_full, xs, 0.0).astype(BF16)
    x_hi = jnp.where(lo_full, 0.0, xs).astype(BF16)
    bm = xbc[:, d_inner:d_inner + d_bc]
    cm = xbc[:, d_inner + d_bc:d_inner + 2 * d_bc]
    bm_bf = bm.astype(BF16)
    cm_bf = cm.astype(BF16)
    pairs_per_group = SSM_HEADS // SSM_GROUPS // 2
    state = state_s[...]
    y_chunks = []
    for c in range(n_chunks):
        cs = slice(c * CHUNK, (c + 1) * CHUNK)
        acum, acum_t, dt_t, te_t, chunk_decay = chunk_terms[c]

        st_lo = jnp.where(lo_full, state, 0.0).astype(BF16)
        st_hi = jnp.where(lo_full, 0.0, state).astype(BF16)

        y_pairs = []
        s_pairs = []
        for j in range(SSM_HEADS // 2):
            ps = slice(j * LANES, (j + 1) * LANES)
            if j % pairs_per_group == 0:
                g = j // pairs_per_group
                ns = slice(g * SSM_STATE, (g + 1) * SSM_STATE)
                bg = bm[cs, ns]
                cg = cm[cs, ns]
                cb = _dot_nt(cm_bf[cs, ns], bm_bf[cs, ns])
                bg_t = bg.T
            y_acc = None
            s_acc = None
            for h, x_sel, st_sel in ((2 * j, x_lo, st_lo), (2 * j + 1, x_hi, st_hi)):
                a_col = jnp.broadcast_to(acum[:, h:h + 1], (CHUNK, CHUNK))
                seg = a_col - acum_t[h:h + 1, :]
                decay = jnp.exp(jnp.where(causal, seg, -jnp.inf))
                w_intra = (cb * decay * dt_t[h:h + 1, :]).astype(BF16)
                c_in = (cg * jnp.exp(a_col)).astype(BF16)
                xh = x_sel[cs, ps]
                lhs = jnp.concatenate([w_intra, c_in], axis=1)
                rhs = jnp.concatenate([xh, st_sel[:, ps]], axis=0)
                yh = _dot(lhs, rhs)
                sh = _dot((bg_t * te_t[h:h + 1, :]).astype(BF16), xh)
                y_acc = yh if y_acc is None else y_acc + yh
                s_acc = sh if s_acc is None else s_acc + sh
            y_pairs.append(y_acc)
            s_pairs.append(s_acc)
        y_chunks.append(jnp.concatenate(y_pairs, axis=1))
        state = state * chunk_decay + jnp.concatenate(s_pairs, axis=1)
    state_s[...] = state

    prev_y_s[...] = jnp.concatenate(y_chunks, axis=0) + xs * dexp_ref[...]

    for g in range(GM_GROUPS):
        gs = slice(g * CHUNK, (g + 1) * CHUNK)
        w_g = jnp.where(causal, ws_ref[g], 0.0).astype(BF16)
        for c in range(n_chunks):
            cs = slice(c * CHUNK, (c + 1) * CHUNK)
            mixed = _dot(w_g, vn[cs, gs]) + bexp_ref[:, gs]
            mixa_s[cs, gs] = (u[cs, gs] * mixed).astype(BF16)


def _mixer0(h, g, w_in, w_out, ln_g, ln_b, w_s, b_s, conv_w, conv_b, dt_bias, a_log, d_skip,
            norm_g, *, tile=256):
    bsz, seq, d_model = h.shape
    d_inner = d_model
    d_bc = SSM_GROUPS * SSM_STATE
    conv_dim = d_inner + 2 * d_bc
    n_main = 3 * d_model + conv_dim
    assert w_in.shape == (d_model, n_main + SSM_HEADS)
    assert seq % tile == 0 and tile % CHUNK == 0 and d_inner == SSM_HEADS * HEADDIM

    pad = LANES - SSM_HEADS
    w_in_p = jnp.pad(w_in, ((0, 0), (0, pad))).astype(BF16)
    row = lambda v: v.reshape(1, -1).astype(F32)
    pad_row = lambda v: jnp.pad(v.astype(F32), (0, pad)).reshape(1, LANES)
    bexp = jnp.repeat(b_s.T.astype(F32), CHUNK, axis=1)
    dexp = jnp.repeat(d_skip.astype(F32), HEADDIM).reshape(1, d_inner)
    hexp = (jnp.arange(LANES)[:, None] == (jnp.arange(d_inner)[None, :] // HEADDIM)).astype(F32)

    tiles_per_seq = seq // tile
    n_tiles = bsz * tiles_per_seq

    def tile_index(i):
        return i // tiles_per_seq, i % tiles_per_seq, 0

    const = lambda shape: pl.BlockSpec(shape, lambda i: (0,) * len(shape))
    kern = functools.partial(_mixer0_kernel, tile=tile, tiles_per_seq=tiles_per_seq,
                             d_model=d_model, d_inner=d_inner)
    return pl.pallas_call(
        kern,
        grid=(n_tiles + 1,),
        in_specs=[
            pl.BlockSpec((1, tile, d_model), lambda i: tile_index(jnp.minimum(i, n_tiles - 1))),
            const((1, d_model)),
            const(w_in_p.shape),
            const(w_out.shape),
            const((1, d_model)), const((1, d_model)),
            const(w_s.shape),
            const(bexp.shape),
            const(conv_w.shape), const((1, conv_dim)),
            const((1, LANES)), const((1, LANES)),
            const((1, d_inner)), const((1, d_inner)),
            const(hexp.shape),
        ],
        out_specs=pl.BlockSpec((1, tile, d_model), lambda i: tile_index(jnp.maximum(i - 1, 0))),
        out_shape=jax.ShapeDtypeStruct(h.shape, F32),
        scratch_shapes=[
            pltpu.VMEM((CONV_CARRY, conv_dim), F32),
            pltpu.VMEM((SSM_STATE, d_inner), F32),
            pltpu.VMEM((tile, d_model), BF16),
            pltpu.VMEM((tile, d_model), F32),
            pltpu.VMEM((tile, d_inner), F32),
            pltpu.VMEM((tile, d_inner), F32),
        ],
        compiler_params=pltpu.CompilerParams(
            dimension_semantics=("arbitrary",),
            vmem_limit_bytes=VMEM_LIMIT),
        name="mixer0",
    )(h, row(g), w_in_p, w_out.astype(BF16), row(ln_g), row(ln_b), w_s.astype(F32), bexp,
      conv_w.astype(F32), row(conv_b), pad_row(dt_bias), pad_row(a_log), dexp, row(norm_g),
      hexp)


def _mlp_kernel(x_ref, g_ref, w_up_ref, w_down_ref, gf_ref, out_ref, *, ff_chunk, final_norm):
    x = x_ref[...]
    y = _rmsnorm(x, g_ref[...]).astype(BF16)
    acc = x
    d_ff = w_up_ref.shape[1]
    for c in range(d_ff // ff_chunk):
        cs = slice(c * ff_chunk, (c + 1) * ff_chunk)
        hid = jnp.maximum(_dot(y, w_up_ref[:, cs]), 0.0)
        acc = acc + _dot((hid * hid).astype(BF16), w_down_ref[cs, :])
    if final_norm:
        acc = _rmsnorm(acc, gf_ref[...])
    out_ref[...] = acc


def _mlp(h, g, w_up, w_down, g_final, *, layer, final_norm, tile=1024, ff_chunk=1024):
    bsz, seq, d_model = h.shape
    n = bsz * seq
    d_ff = w_up.shape[2]
    assert n % tile == 0 and d_ff % ff_chunk == 0
    x2 = h.reshape(n, d_model)
    const = lambda shape: pl.BlockSpec(shape, lambda i: (0,) * len(shape))
    of_layer = lambda shape: pl.BlockSpec((None,) + shape, lambda i: (layer, 0, 0),
                                          pipeline_mode=pl.Buffered(1))
    out = pl.pallas_call(
        functools.partial(_mlp_kernel, ff_chunk=ff_chunk, final_norm=final_norm),
        grid=(n // tile,),
        in_specs=[
            pl.BlockSpec((tile, d_model), lambda i: (i, 0)),
            const((1, d_model)),
            of_layer((d_model, d_ff)),
            of_layer((d_ff, d_model)),
            const((1, d_model)),
        ],
        out_specs=pl.BlockSpec((tile, d_model), lambda i: (i, 0)),
        out_shape=jax.ShapeDtypeStruct((n, d_model), F32),
        compiler_params=pltpu.CompilerParams(
            dimension_semantics=("arbitrary",), vmem_limit_bytes=VMEM_LIMIT),
        name="mlp_final" if final_norm else "mlp",
    )(x2, g.reshape(1, -1).astype(F32), w_up, w_down, g_final.reshape(1, -1).astype(F32))
    return out.reshape(bsz, seq, d_model)


def _attn_kernel(sinks_ref, x_ref, g_ref, wq_ref, bq_ref, wk_ref, bk_ref, wvt_ref, bvt_ref,
                 wo_ref, bo_ref, out_ref, q_s, k_s, vt_s, att_s, *, tile):
    n_blocks = tile // CHUNK
    grp_rows = PAIRS_PER_KV * CHUNK
    t_idx = pl.program_id(1)

    @pl.when(t_idx == 0)
    def _():
        k_s[:, 0:CHUNK, :] = jnp.zeros((4, CHUNK, LANES), BF16)
        vt_s[:, 0] = jnp.zeros((4, LANES, CHUNK), BF16)

    @pl.when(t_idx != 0)
    def _():
        k_s[:, 0:CHUNK, :] = k_s[:, tile:tile + CHUNK, :]
        vt_s[:, 0] = vt_s[:, n_blocks]

    x = x_ref[0]
    y = _rmsnorm(x, g_ref[...]).astype(BF16)
    q = ((_dot(y, wq_ref[...]) + bq_ref[...]) * (HEADDIM ** -0.5 * LOG2E)).astype(BF16)
    for i in range(n_blocks):
        for j in range(ATTN_PAIRS):
            q_s[i, j * CHUNK:(j + 1) * CHUNK, :] = q[i * CHUNK:(i + 1) * CHUNK,
                                                     j * LANES:(j + 1) * LANES]

    k = _dot(y, wk_ref[...]) + bk_ref[...]
    k_swp = pltpu.roll(k, HALF, axis=1)
    lo_lane = lax.broadcasted_iota(jnp.int32, (1, LANES), 1) < HALF
    k_variants = (jnp.where(lo_lane, k, 0.0), jnp.where(lo_lane, 0.0, k_swp),
                  jnp.where(lo_lane, k_swp, 0.0), jnp.where(lo_lane, 0.0, k))
    for g, val in enumerate(k_variants):
        k_s[g, CHUNK:CHUNK + tile, :] = val.astype(BF16)

    bvt = jnp.concatenate([bvt_ref[...]] * n_blocks, axis=1)
    vt = _dot_nt(wvt_ref[...], y) + bvt
    vt_swp = pltpu.roll(vt, HALF, axis=0)
    lo_row = lax.broadcasted_iota(jnp.int32, (LANES, 1), 0) < HALF
    v_variants = (jnp.where(lo_row, vt, 0.0), jnp.where(lo_row, 0.0, vt_swp),
                  jnp.where(lo_row, vt_swp, 0.0), jnp.where(lo_row, 0.0, vt))
    for g, val in enumerate(v_variants):
        for b in range(n_blocks):
            vt_s[g, 1 + b] = val[:, b * CHUNK:(b + 1) * CHUNK].astype(BF16)

    si = lax.broadcasted_iota(jnp.int32, (2 * CHUNK, CHUNK), 0)
    qi = lax.broadcasted_iota(jnp.int32, (2 * CHUNK, CHUNK), 1)
    valid = (si > qi) & (si <= qi + CHUNK)
    neg_inf = jnp.float32(-jnp.inf)
    bias = jnp.where(valid, 0.0, neg_inf)
    bias_first = jnp.where(valid & (si >= CHUNK), 0.0, neg_inf)

    n_var = 2 * ATTN_KV_HEADS

    def scores_t(i, g):
        kv = g // 2
        return _dot_nt(k_s[g, i * CHUNK:(i + 2) * CHUNK, :],
                       q_s[i, kv * grp_rows:(kv + 1) * grp_rows, :])

    st_next = scores_t(0, 0)
    for i in range(n_blocks):
        blk_bias = jnp.where(t_idx == 0, bias_first, bias) if i == 0 else bias
        for kv in range(ATTN_KV_HEADS):
            o_acc = None
            for par in range(2):
                g = 2 * kv + par
                st_all = st_next
                if g + 1 < n_var:
                    st_next = scores_t(i, g + 1)
                elif i + 1 < n_blocks:
                    st_next = scores_t(i + 1, 0)
                p_cols = []
                inv_cols = []
                for jj in range(PAIRS_PER_KV):
                    head = 2 * (kv * PAIRS_PER_KV + jj) + par
                    s = st_all[:, jj * CHUNK:(jj + 1) * CHUNK] + blk_bias
                    sink = sinks_ref[head] * LOG2E
                    m = jnp.maximum(jnp.max(s, axis=0, keepdims=True), sink)
                    p = jnp.exp2(s - m)
                    denom = jnp.sum(p, axis=0, keepdims=True) + jnp.exp2(sink - m)
                    p_cols.append(p.astype(BF16))
                    inv_cols.append(1.0 / denom)
                v_band = jnp.concatenate([vt_s[g, i], vt_s[g, i + 1]], axis=1)
                ot = _dot(v_band, jnp.concatenate(p_cols, axis=1))
                ot = ot * jnp.concatenate(inv_cols, axis=1)
                o_acc = ot if o_acc is None else o_acc + ot
            for jj in range(PAIRS_PER_KV):
                j = kv * PAIRS_PER_KV + jj
                att_s[i * CHUNK:(i + 1) * CHUNK, j * LANES:(j + 1) * LANES] = (
                    o_acc[:, jj * CHUNK:(jj + 1) * CHUNK].T.astype(BF16))

    out_ref[0] = x + _dot(att_s[...], wo_ref[...]) + bo_ref[...]


def _attn(h, g, w_qkv, b_qkv, w_o, b_o, sinks, *, tile=1024):
    bsz, seq, d_model = h.shape
    d_q = ATTN_HEADS * HEADDIM
    d_kv = ATTN_KV_HEADS * HEADDIM
    assert w_qkv.shape == (d_model, d_q + 2 * d_kv) and d_kv == LANES
    assert seq % tile == 0 and tile % CHUNK == 0
    n_blocks = tile // CHUNK
    row = lambda v: v.reshape(1, -1).astype(F32)
    w_q, w_k, w_v = w_qkv[:, :d_q], w_qkv[:, d_q:d_q + d_kv], w_qkv[:, d_q + d_kv:]
    b_q, b_k, b_v = b_qkv[:d_q], b_qkv[d_q:d_q + d_kv], b_qkv[d_q + d_kv:]
    b_vt = jnp.broadcast_to(b_v.astype(F32)[:, None], (d_kv, CHUNK))
    const = lambda shape: pl.BlockSpec(shape, lambda b, t, s: (0,) * len(shape))
    grid_spec = pltpu.PrefetchScalarGridSpec(
        num_scalar_prefetch=1,
        grid=(bsz, seq // tile),
        in_specs=[
            pl.BlockSpec((1, tile, d_model), lambda b, t, s: (b, t, 0)),
            const((1, d_model)),
            const((d_model, d_q)), const((1, d_q)),
            const((d_model, d_kv)), const((1, d_kv)),
            const((d_kv, d_model)), const((d_kv, CHUNK)),
            const((d_q, d_model)), const((1, d_model)),
        ],
        out_specs=pl.BlockSpec((1, tile, d_model), lambda b, t, s: (b, t, 0)),
        scratch_shapes=[
            pltpu.VMEM((n_blocks, ATTN_PAIRS * CHUNK, LANES), BF16),
            pltpu.VMEM((4, tile + CHUNK, LANES), BF16),
            pltpu.VMEM((4, n_blocks + 1, LANES, CHUNK), BF16),
            pltpu.VMEM((tile, d_q), BF16),
        ],
    )
    return pl.pallas_call(
        functools.partial(_attn_kernel, tile=tile),
        grid_spec=grid_spec,
        out_shape=jax.ShapeDtypeStruct(h.shape, F32),
        compiler_params=pltpu.CompilerParams(
            dimension_semantics=("arbitrary", "arbitrary"),
            vmem_limit_bytes=VMEM_LIMIT),
        name="attn",
    )(sinks.astype(F32), h, row(g), w_q.astype(BF16), row(b_q), w_k.astype(BF16), row(b_k),
      w_v.T.astype(BF16), b_vt, w_o.astype(BF16), row(b_o))


def kernel(x, norm_mix_g, norm_mlp_g, final_norm_g, w_in_even, w_out_even, gm_ln_g, gm_ln_b,
           gm_w_s, gm_b_s, ssm_conv_w, ssm_conv_b, ssm_dt_bias, ssm_a_log, ssm_d, ssm_norm_g,
           w_qkv, b_qkv, w_o, b_o, attn_sinks, w_up, w_down):
    depth = norm_mix_g.shape[0]
    w_up_bf = w_up.astype(BF16)
    w_down_bf = w_down.astype(BF16)
    h = x
    for i in range(depth):
        j = i // 2
        if i % 2 == 0:
            h = _mixer0(h, norm_mix_g[i], w_in_even[j], w_out_even[j], gm_ln_g[j], gm_ln_b[j],
                        gm_w_s[j], gm_b_s[j], ssm_conv_w[j], ssm_conv_b[j], ssm_dt_bias[j],
                        ssm_a_log[j], ssm_d[j], ssm_norm_g[j])
        else:
            h = _attn(h, norm_mix_g[i], w_qkv[j], b_qkv[j], w_o[j], b_o[j], attn_sinks[j])
        h = _mlp(h, norm_mlp_g[i], w_up_bf, w_down_bf, final_norm_g, layer=i,
                 final_norm=(i == depth - 1))
    return h
```

```python
import functools

import jax
import jax.numpy as jnp
from jax import lax
from jax.experimental import pallas as pl
from jax.experimental.pallas import tpu as pltpu

F32 = jnp.float32
BF16 = jnp.bfloat16

RMS_EPS = 1e-5
LN_EPS = 1e-5

LANES = 128
CHUNK = 128
HEADDIM = 64
HALF = LANES // 2
LOG2E = 1.4426950408889634

GM_GROUPS = 8
SSM_HEADS = 16
SSM_GROUPS = 4
SSM_STATE = 128
SSM_CONV = 4
SLAB = 512
CONV_CARRY = 8

ATTN_HEADS = 16
ATTN_KV_HEADS = 2
ATTN_PAIRS = ATTN_HEADS // 2
PAIRS_PER_KV = ATTN_PAIRS // ATTN_KV_HEADS

VMEM_LIMIT = 56 * 1024 * 1024


def _dot(a, b):
    return jnp.dot(a, b, preferred_element_type=F32)


def _dot_nt(a, b):
    return lax.dot_general(a, b, (((1,), (1,)), ((), ())), preferred_element_type=F32)


def _split3(x):
    hi = x.astype(BF16)
    r1 = x - hi.astype(F32)
    mid = r1.astype(BF16)
    lo = (r1 - mid.astype(F32)).astype(BF16)
    return hi, mid, lo


def _dot_sel_lhs(sel, b):
    sel = sel.astype(BF16)
    hi, mid, lo = _split3(b)
    return _dot(sel, hi) + _dot(sel, mid) + _dot(sel, lo)


def _dot_sel_rhs(a, sel):
    sel = sel.astype(BF16)
    hi, mid, lo = _split3(a)
    return _dot(hi, sel) + _dot(mid, sel) + _dot(lo, sel)


def _rmsnorm(x, g):
    ms = jnp.mean(x * x, axis=-1, keepdims=True)
    return x * lax.rsqrt(ms + RMS_EPS) * g


def _gelu_tanh(x):
    c = 0.7978845608028654
    half = 0.5 * x
    return half + half * jnp.tanh(x * (c + (0.044715 * c) * (x * x)))


def _silu(x):
    half = 0.5 * x
    return half + half * jnp.tanh(half)


def _softplus(x):
    return jnp.maximum(x, 0.0) + jnp.log1p(jnp.exp(-jnp.abs(x)))


def _mixer0_kernel(x_ref, g_ref, w_in_ref, w_out_ref, lng_ref, lnb_ref, ws_ref, bexp_ref,
                   convw_ref, convb_ref, dtb_ref, alog_ref, dexp_ref, ng_ref, hexp_ref,
                   out_ref, carry_s, state_s, mixa_s, prev_x_s, prev_y_s, prev_z_s,
                   *, tile, tiles_per_seq, d_model, d_inner):
    n_chunks = tile // CHUNK
    d_bc = SSM_GROUPS * SSM_STATE
    step = pl.program_id(0)

    @pl.when(step == 0)
    def _():
        mixa_s[...] = jnp.zeros_like(mixa_s)
        prev_x_s[...] = jnp.zeros_like(prev_x_s)
        prev_y_s[...] = jnp.zeros_like(prev_y_s)
        prev_z_s[...] = jnp.zeros_like(prev_z_s)

    @pl.when(step % tiles_per_seq == 0)
    def _():
        state_s[...] = jnp.zeros_like(state_s)
        carry_s[...] = jnp.zeros_like(carry_s)

    row = lax.broadcasted_iota(jnp.int32, (CHUNK, CHUNK), 0)
    col = lax.broadcasted_iota(jnp.int32, (CHUNK, CHUNK), 1)
    causal = row >= col
    o_u, o_v, o_z, o_x = 0, d_model, 2 * d_model, 2 * d_model + d_inner
    o_dt = o_x + d_inner + 2 * d_bc
    conv_dim = d_inner + 2 * d_bc
    n_slabs_d = d_model // SLAB
    gw = d_inner // SSM_GROUPS


    mixa_prev = mixa_s[...]
    out_acc = [_dot(mixa_prev, w_out_ref[0:d_model, n * SLAB:(n + 1) * SLAB])
               for n in range(n_slabs_d)]
    x = x_ref[0]
    y = _rmsnorm(x, g_ref[...]).astype(BF16)

    def in_proj(lo, width=SLAB):
        return _dot(y, w_in_ref[:, lo:lo + width])

    def conv_silu(raw, lo):
        cols = slice(lo, lo + SLAB)
        head16 = jnp.concatenate([carry_s[:, cols], raw[0:CONV_CARRY, :]], axis=0)
        w_last = convw_ref[SSM_CONV - 1:SSM_CONV, cols]
        conv = convb_ref[:, cols] + w_last * raw
        conv8 = convb_ref[:, cols] + w_last * raw[0:CONV_CARRY, :]
        for k in range(1, SSM_CONV):
            w_k = convw_ref[SSM_CONV - 1 - k:SSM_CONV - k, cols]
            conv = conv + w_k * pltpu.roll(raw, k, axis=0)
            conv8 = conv8 + w_k * pltpu.roll(head16, k, axis=0)[CONV_CARRY:2 * CONV_CARRY, :]
        carry_s[:, cols] = raw[tile - CONV_CARRY:tile, :]
        return _silu(jnp.concatenate([conv8, conv[CONV_CARRY:, :]], axis=0))

    xbc_raw = [in_proj(o_x)]
    yv = prev_y_s[...] * _silu(prev_z_s[...])
    xbc_raw.append(in_proj(o_x + SLAB))
    b_out = jnp.concatenate(
        [_rmsnorm(yv[:, g * gw:(g + 1) * gw], ng_ref[:, g * gw:(g + 1) * gw]).astype(BF16)
         for g in range(SSM_GROUPS)], axis=1)
    xbc_act = []
    for n in range(2, conv_dim // SLAB):
        xbc_raw.append(in_proj(o_x + n * SLAB))
        xbc_act.append(conv_silu(xbc_raw[n - 2], (n - 2) * SLAB))
    dt_raw = in_proj(o_dt, LANES)
    u_raw = [in_proj(o_u)]
    xbc_act.append(conv_silu(xbc_raw[-2], conv_dim - 2 * SLAB))
    u_raw.append(in_proj(o_u + SLAB))
    xbc_act.append(conv_silu(xbc_raw[-1], conv_dim - SLAB))
    xbc = jnp.concatenate(xbc_act, axis=1)

    v_raw = [in_proj(o_v)]
    u = [_gelu_tanh(u_raw[0])]
    dt = _softplus(dt_raw + dtb_ref[...])
    a = dt * (-jnp.exp(alog_ref[...]))
    v_raw.append(in_proj(o_v + SLAB))
    u.append(_gelu_tanh(u_raw[1]))
    u = jnp.concatenate(u, axis=1)

    tril = jnp.where(causal, 1.0, 0.0).astype(F32)
    acums = [_dot_sel_lhs(tril, a[c * CHUNK:(c + 1) * CHUNK, :]) for c in range(n_chunks)]

    v = []
    for n in range(n_slabs_d):
        out_acc[n] = out_acc[n] + _dot(b_out, w_out_ref[d_model:, n * SLAB:(n + 1) * SLAB])
        v.append(_gelu_tanh(v_raw[n]))
    out_ref[0] = prev_x_s[...] + jnp.concatenate(out_acc, axis=1)
    v = jnp.concatenate(v, axis=1)

    chunk_terms = []
    for c in range(n_chunks):
        acum = acums[c]
        dtc = dt[c * CHUNK:(c + 1) * CHUNK, :]
        last = acum[CHUNK - 1:CHUNK, :]
        te_t = (jnp.exp(last - acum) * dtc).T
        decay8 = _dot_sel_rhs(jnp.exp(acum[CHUNK - 8:CHUNK, :]), hexp_ref[...])
        chunk_terms.append((acum, acum.T, dtc.T, te_t, decay8[7:8, :]))

    z_raw = [in_proj(o_z)]
    mu = jnp.mean(v, axis=-1, keepdims=True)
    vc = v - mu
    var = jnp.mean(vc * vc, axis=-1, keepdims=True)
    z_raw.append(in_proj(o_z + SLAB))
    vn = (vc * lax.rsqrt(var + LN_EPS) * lng_ref[...] + lnb_ref[...]).astype(BF16)
    prev_z_s[...] = jnp.concatenate(z_raw, axis=1)
    prev_x_s[...] = x

    xs = xbc[:, 0:d_inner]
    lo_full = lax.broadcasted_iota(jnp.int32, (1, d_inner), 1) % LANES < HALF
    x_lo = jnp.where(lo_full, xs, 0.0).astype(BF16)
    x_hi = jnp.where(lo_full, 0.0, xs).astype(BF16)
    bm = xbc[:, d_inner:d_inner + d_bc]
    cm = xbc[:, d_inner + d_bc:d_inner + 2 * d_bc]
    bm_bf = bm.astype(BF16)
    cm_bf = cm.astype(BF16)
    pairs_per_group = SSM_HEADS // SSM_GROUPS // 2
    state = state_s[...]
    y_chunks = []
    for c in range(n_chunks):
        cs = slice(c * CHUNK, (c + 1) * CHUNK)
        acum, acum_t, dt_t, te_t, chunk_decay = chunk_terms[c]

        st_lo = jnp.where(lo_full, state, 0.0).astype(BF16)
        st_hi = jnp.where(lo_full, 0.0, state).astype(BF16)

        y_pairs = []
        s_pairs = []
        for j in range(SSM_HEADS // 2):
            ps = slice(j * LANES, (j + 1) * LANES)
            if j % pairs_per_group == 0:
                g = j // pairs_per_group
                ns = slice(g * SSM_STATE, (g + 1) * SSM_STATE)
                bg = bm[cs, ns]
                cg = cm[cs, ns]
                cb = _dot_nt(cm_bf[cs, ns], bm_bf[cs, ns])
                bg_t = bg.T
            y_acc = None
            s_acc = None
            for h, x_sel, st_sel in ((2 * j, x_lo, st_lo), (2 * j + 1, x_hi, st_hi)):
                a_col = jnp.broadcast_to(acum[:, h:h + 1], (CHUNK, CHUNK))
                seg = a_col - acum_t[h:h + 1, :]
                decay = jnp.exp(jnp.where(causal, seg, -jnp.inf))
                w_intra = (cb * decay * dt_t[h:h + 1, :]).astype(BF16)
                c_in = (cg * jnp.exp(a_col)).astype(BF16)
                xh = x_sel[cs, ps]
                lhs = jnp.concatenate([w_intra, c_in], axis=1)
                rhs = jnp.concatenate([xh, st_sel[:, ps]], axis=0)
                yh = _dot(lhs, rhs)
                sh = _dot((bg_t * te_t[h:h + 1, :]).astype(BF16), xh)
                y_acc = yh if y_acc is None else y_acc + yh
                s_acc = sh if s_acc is None else s_acc + sh
            y_pairs.append(y_acc)
            s_pairs.append(s_acc)
        y_chunks.append(jnp.concatenate(y_pairs, axis=1))
        state = state * chunk_decay + jnp.concatenate(s_pairs, axis=1)
    state_s[...] = state

    prev_y_s[...] = jnp.concatenate(y_chunks, axis=0) + xs * dexp_ref[...]

    for g in range(GM_GROUPS):
        gs = slice(g * CHUNK, (g + 1) * CHUNK)
        w_g = jnp.where(causal, ws_ref[g], 0.0).astype(BF16)
        for c in range(n_chunks):
            cs = slice(c * CHUNK, (c + 1) * CHUNK)
            mixed = _dot(w_g, vn[cs, gs]) + bexp_ref[:, gs]
            mixa_s[cs, gs] = (u[cs, gs] * mixed).astype(BF16)


def _mixer0(h, g, w_in, w_out, ln_g, ln_b, w_s, b_s, conv_w, conv_b, dt_bias, a_log, d_skip,
            norm_g, *, tile=512):
    bsz, seq, d_model = h.shape
    d_inner = d_model
    d_bc = SSM_GROUPS * SSM_STATE
    conv_dim = d_inner + 2 * d_bc
    n_main = 3 * d_model + conv_dim
    assert w_in.shape == (d_model, n_main + SSM_HEADS)
    assert seq % tile == 0 and tile % CHUNK == 0 and d_inner == SSM_HEADS * HEADDIM

    pad = LANES - SSM_HEADS
    w_in_p = jnp.concatenate([w_in.astype(BF16), jnp.zeros((d_model, pad), BF16)], axis=1)
    row = lambda v: v.reshape(1, -1).astype(F32)
    pad_row = lambda v: jnp.pad(v.astype(F32), (0, pad)).reshape(1, LANES)
    bexp = jnp.repeat(b_s.T.astype(F32), CHUNK, axis=1)
    dexp = jnp.repeat(d_skip.astype(F32), HEADDIM).reshape(1, d_inner)
    hexp = (jnp.arange(LANES)[:, None] == (jnp.arange(d_inner)[None, :] // HEADDIM)).astype(F32)

    tiles_per_seq = seq // tile
    n_tiles = bsz * tiles_per_seq

    def tile_index(i):
        return i // tiles_per_seq, i % tiles_per_seq, 0

    const = lambda shape: pl.BlockSpec(shape, lambda i: (0,) * len(shape))
    kern = functools.partial(_mixer0_kernel, tile=tile, tiles_per_seq=tiles_per_seq,
                             d_model=d_model, d_inner=d_inner)
    return pl.pallas_call(
        kern,
        grid=(n_tiles + 1,),
        in_specs=[
            pl.BlockSpec((1, tile, d_model), lambda i: tile_index(jnp.minimum(i, n_tiles - 1))),
            const((1, d_model)),
            pl.BlockSpec(w_in_p.shape, lambda i: (0, 0), pipeline_mode=pl.Buffered(1)),
            pl.BlockSpec(w_out.shape, lambda i: (0, 0), pipeline_mode=pl.Buffered(1)),
            const((1, d_model)), const((1, d_model)),
            const(w_s.shape),
            const(bexp.shape),
            const(conv_w.shape), const((1, conv_dim)),
            const((1, LANES)), const((1, LANES)),
            const((1, d_inner)), const((1, d_inner)),
            const(hexp.shape),
        ],
        out_specs=pl.BlockSpec((1, tile, d_model), lambda i: tile_index(jnp.maximum(i - 1, 0))),
        out_shape=jax.ShapeDtypeStruct(h.shape, F32),
        scratch_shapes=[
            pltpu.VMEM((CONV_CARRY, conv_dim), F32),
            pltpu.VMEM((SSM_STATE, d_inner), F32),
            pltpu.VMEM((tile, d_model), BF16),
            pltpu.VMEM((tile, d_model), F32),
            pltpu.VMEM((tile, d_inner), F32),
            pltpu.VMEM((tile, d_inner), F32),
        ],
        compiler_params=pltpu.CompilerParams(
            dimension_semantics=("arbitrary",),
            vmem_limit_bytes=VMEM_LIMIT),
        name="mixer0",
    )(h, row(g), w_in_p, w_out.astype(BF16), row(ln_g), row(ln_b), w_s.astype(F32), bexp,
      conv_w.astype(F32), row(conv_b), pad_row(dt_bias), pad_row(a_log), dexp, row(norm_g),
      hexp)


def _mlp_kernel(x_ref, g_ref, w_up_ref, w_down_ref, gf_ref, out_ref, *, ff_chunk, final_norm):
    x = x_ref[...]
    y = _rmsnorm(x, g_ref[...]).astype(BF16)
    acc = x
    d_ff = w_up_ref.shape[1]
    for c in range(d_ff // ff_chunk):
        cs = slice(c * ff_chunk, (c + 1) * ff_chunk)
        hid = jnp.maximum(_dot(y, w_up_ref[:, cs]), 0.0)
        acc = acc + _dot((hid * hid).astype(BF16), w_down_ref[cs, :])
    if final_norm:
        acc = _rmsnorm(acc, gf_ref[...])
    out_ref[...] = acc


def _mlp(h, g, w_up, w_down, g_final, *, layer, final_norm, tile=1024, ff_chunk=1024):
    bsz, seq, d_model = h.shape
    n = bsz * seq
    d_ff = w_up.shape[2]
    assert n % tile == 0 and d_ff % ff_chunk == 0
    x2 = h.reshape(n, d_model)
    const = lambda shape: pl.BlockSpec(shape, lambda i: (0,) * len(shape))
    of_layer = lambda shape: pl.BlockSpec((None,) + shape, lambda i: (layer, 0, 0),
                                          pipeline_mode=pl.Buffered(1))
    out = pl.pallas_call(
        functools.partial(_mlp_kernel, ff_chunk=ff_chunk, final_norm=final_norm),
        grid=(n // tile,),
        in_specs=[
            pl.BlockSpec((tile, d_model), lambda i: (i, 0)),
            const((1, d_model)),
            of_layer((d_model, d_ff)),
            of_layer((d_ff, d_model)),
            const((1, d_model)),
        ],
        out_specs=pl.BlockSpec((tile, d_model), lambda i: (i, 0)),
        out_shape=jax.ShapeDtypeStruct((n, d_model), F32),
        compiler_params=pltpu.CompilerParams(
            dimension_semantics=("arbitrary",), vmem_limit_bytes=VMEM_LIMIT),
        name="mlp_final" if final_norm else "mlp",
    )(x2, g.reshape(1, -1).astype(F32), w_up, w_down, g_final.reshape(1, -1).astype(F32))
    return out.reshape(bsz, seq, d_model)


def _attn_kernel(sinks_ref, x_ref, g_ref, wq_ref, bq_ref, wk_ref, bk_ref, wvt_ref, bvt_ref,
                 wo_ref, bo_ref, out_ref, q_s, k_s, vt_s, att_s, *, tile):
    n_blocks = tile // CHUNK
    grp_rows = PAIRS_PER_KV * CHUNK
    t_idx = pl.program_id(1)

    @pl.when(t_idx == 0)
    def _():
        k_s[:, 0:CHUNK, :] = jnp.zeros((4, CHUNK, LANES), BF16)
        vt_s[:, 0] = jnp.zeros((4, LANES, CHUNK), BF16)

    @pl.when(t_idx != 0)
    def _():
        k_s[:, 0:CHUNK, :] = k_s[:, tile:tile + CHUNK, :]
        vt_s[:, 0] = vt_s[:, n_blocks]

    x = x_ref[0]
    y = _rmsnorm(x, g_ref[...]).astype(BF16)
    q = ((_dot(y, wq_ref[...]) + bq_ref[...]) * (HEADDIM ** -0.5 * LOG2E)).astype(BF16)
    for i in range(n_blocks):
        for j in range(ATTN_PAIRS):
            q_s[i, j * CHUNK:(j + 1) * CHUNK, :] = q[i * CHUNK:(i + 1) * CHUNK,
                                                     j * LANES:(j + 1) * LANES]

    k = _dot(y, wk_ref[...]) + bk_ref[...]
    k_swp = pltpu.roll(k, HALF, axis=1)
    lo_lane = lax.broadcasted_iota(jnp.int32, (1, LANES), 1) < HALF
    k_variants = (jnp.where(lo_lane, k, 0.0), jnp.where(lo_lane, 0.0, k_swp),
                  jnp.where(lo_lane, k_swp, 0.0), jnp.where(lo_lane, 0.0, k))
    for g, val in enumerate(k_variants):
        k_s[g, CHUNK:CHUNK + tile, :] = val.astype(BF16)

    bvt = jnp.concatenate([bvt_ref[...]] * n_blocks, axis=1)
    vt = _dot_nt(wvt_ref[...], y) + bvt
    vt_swp = pltpu.roll(vt, HALF, axis=0)
    lo_row = lax.broadcasted_iota(jnp.int32, (LANES, 1), 0) < HALF
    v_variants = (jnp.where(lo_row, vt, 0.0), jnp.where(lo_row, 0.0, vt_swp),
                  jnp.where(lo_row, vt_swp, 0.0), jnp.where(lo_row, 0.0, vt))
    for g, val in enumerate(v_variants):
        for b in range(n_blocks):
            vt_s[g, 1 + b] = val[:, b * CHUNK:(b + 1) * CHUNK].astype(BF16)

    si = lax.broadcasted_iota(jnp.int32, (2 * CHUNK, CHUNK), 0)
    qi = lax.broadcasted_iota(jnp.int32, (2 * CHUNK, CHUNK), 1)
    valid = (si > qi) & (si <= qi + CHUNK)
    neg_inf = jnp.float32(-jnp.inf)
    bias = jnp.where(valid, 0.0, neg_inf)
    bias_first = jnp.where(valid & (si >= CHUNK), 0.0, neg_inf)

    n_var = 2 * ATTN_KV_HEADS

    def scores_t(i, g):
        kv = g // 2
        return _dot_nt(k_s[g, i * CHUNK:(i + 2) * CHUNK, :],
                       q_s[i, kv * grp_rows:(kv + 1) * grp_rows, :])

    st_next = scores_t(0, 0)
    for i in range(n_blocks):
        blk_bias = jnp.where(t_idx == 0, bias_first, bias) if i == 0 else bias
        for kv in range(ATTN_KV_HEADS):
            o_acc = None
            for par in range(2):
                g = 2 * kv + par
                st_all = st_next
                if g + 1 < n_var:
                    st_next = scores_t(i, g + 1)
                elif i + 1 < n_blocks:
                    st_next = scores_t(i + 1, 0)
                p_cols = []
                inv_cols = []
                for jj in range(PAIRS_PER_KV):
                    head = 2 * (kv * PAIRS_PER_KV + jj) + par
                    s = st_all[:, jj * CHUNK:(jj + 1) * CHUNK] + blk_bias
                    sink = sinks_ref[head] * LOG2E
                    m = jnp.maximum(jnp.max(s, axis=0, keepdims=True), sink)
                    p = jnp.exp2(s - m)
                    denom = jnp.sum(p, axis=0, keepdims=True) + jnp.exp2(sink - m)
                    p_cols.append(p.astype(BF16))
                    inv_cols.append(1.0 / denom)
                v_band = jnp.concatenate([vt_s[g, i], vt_s[g, i + 1]], axis=1)
                ot = _dot(v_band, jnp.concatenate(p_cols, axis=1))
                ot = ot * jnp.concatenate(inv_cols, axis=1)
                o_acc = ot if o_acc is None else o_acc + ot
            for jj in range(PAIRS_PER_KV):
                j = kv * PAIRS_PER_KV + jj
                att_s[i * CHUNK:(i + 1) * CHUNK, j * LANES:(j + 1) * LANES] = (
                    o_acc[:, jj * CHUNK:(jj + 1) * CHUNK].T.astype(BF16))

    out_ref[0] = x + _dot(att_s[...], wo_ref[...]) + bo_ref[...]


def _attn(h, g, w_qkv, b_qkv, w_o, b_o, sinks, *, tile=1024):
    bsz, seq, d_model = h.shape
    d_q = ATTN_HEADS * HEADDIM
    d_kv = ATTN_KV_HEADS * HEADDIM
    assert w_qkv.shape == (d_model, d_q + 2 * d_kv) and d_kv == LANES
    assert seq % tile == 0 and tile % CHUNK == 0
    n_blocks = tile // CHUNK
    row = lambda v: v.reshape(1, -1).astype(F32)
    w_q, w_k, w_v = w_qkv[:, :d_q], w_qkv[:, d_q:d_q + d_kv], w_qkv[:, d_q + d_kv:]
    b_q, b_k, b_v = b_qkv[:d_q], b_qkv[d_q:d_q + d_kv], b_qkv[d_q + d_kv:]
    b_vt = jnp.broadcast_to(b_v.astype(F32)[:, None], (d_kv, CHUNK))
    const = lambda shape: pl.BlockSpec(shape, lambda b, t, s: (0,) * len(shape))
    grid_spec = pltpu.PrefetchScalarGridSpec(
        num_scalar_prefetch=1,
        grid=(bsz, seq // tile),
        in_specs=[
            pl.BlockSpec((1, tile, d_model), lambda b, t, s: (b, t, 0)),
            const((1, d_model)),
            const((d_model, d_q)), const((1, d_q)),
            const((d_model, d_kv)), const((1, d_kv)),
            const((d_kv, d_model)), const((d_kv, CHUNK)),
            const((d_q, d_model)), const((1, d_model)),
        ],
        out_specs=pl.BlockSpec((1, tile, d_model), lambda b, t, s: (b, t, 0)),
        scratch_shapes=[
            pltpu.VMEM((n_blocks, ATTN_PAIRS * CHUNK, LANES), BF16),
            pltpu.VMEM((4, tile + CHUNK, LANES), BF16),
            pltpu.VMEM((4, n_blocks + 1, LANES, CHUNK), BF16),
            pltpu.VMEM((tile, d_q), BF16),
        ],
    )
    return pl.pallas_call(
        functools.partial(_attn_kernel, tile=tile),
        grid_spec=grid_spec,
        out_shape=jax.ShapeDtypeStruct(h.shape, F32),
        compiler_params=pltpu.CompilerParams(
            dimension_semantics=("arbitrary", "arbitrary"),
            vmem_limit_bytes=VMEM_LIMIT),
        name="attn",
    )(sinks.astype(F32), h, row(g), w_q.astype(BF16), row(b_q), w_k.astype(BF16), row(b_k),
      w_v.T.astype(BF16), b_vt, w_o.astype(BF16), row(b_o))


def kernel(x, norm_mix_g, norm_mlp_g, final_norm_g, w_in_even, w_out_even, gm_ln_g, gm_ln_b,
           gm_w_s, gm_b_s, ssm_conv_w, ssm_conv_b, ssm_dt_bias, ssm_a_log, ssm_d, ssm_norm_g,
           w_qkv, b_qkv, w_o, b_o, attn_sinks, w_up, w_down):
    depth = norm_mix_g.shape[0]
    w_up_bf = w_up.astype(BF16)
    w_down_bf = w_down.astype(BF16)
    h = x
    for i in range(depth):
        j = i // 2
        if i % 2 == 0:
            h = _mixer0(h, norm_mix_g[i], w_in_even[j], w_out_even[j], gm_ln_g[j], gm_ln_b[j],
                        gm_w_s[j], gm_b_s[j], ssm_conv_w[j], ssm_conv_b[j], ssm_dt_bias[j],
                        ssm_a_log[j], ssm_d[j], ssm_norm_g[j])
        else:
            h = _attn(h, norm_mix_g[i], w_qkv[j], b_qkv[j], w_o[j], b_o[j], attn_sinks[j])
        h = _mlp(h, norm_mlp_g[i], w_up_bf, w_down_bf, final_norm_g, layer=i,
                 final_norm=(i == depth - 1))
    return h
```

```python
import functools

import jax
import jax.numpy as jnp
from jax import lax
from jax.experimental import pallas as pl
from jax.experimental.pallas import tpu as pltpu

F32 = jnp.float32
BF16 = jnp.bfloat16

RMS_EPS = 1e-5
LN_EPS = 1e-5

LANES = 128
CHUNK = 128
HEADDIM = 64
HALF = LANES // 2
LOG2E = 1.4426950408889634

GM_GROUPS = 8
SSM_HEADS = 16
SSM_GROUPS = 4
SSM_STATE = 128
SSM_CONV = 4
SLAB = 512
CONV_CARRY = 8

ATTN_HEADS = 16
ATTN_KV_HEADS = 2
ATTN_PAIRS = ATTN_HEADS // 2
PAIRS_PER_KV = ATTN_PAIRS // ATTN_KV_HEADS
ATTN_PARTS = 4

VMEM_LIMIT = 56 * 1024 * 1024


def _dot(a, b):
    return jnp.dot(a, b, preferred_element_type=F32)


def _dot_nt(a, b):
    return lax.dot_general(a, b, (((1,), (1,)), ((), ())), preferred_element_type=F32)


def _split3(x):
    hi = x.astype(BF16)
    r1 = x - hi.astype(F32)
    mid = r1.astype(BF16)
    lo = (r1 - mid.astype(F32)).astype(BF16)
    return hi, mid, lo


def _dot_sel_lhs(sel, b):
    sel = sel.astype(BF16)
    hi, mid, lo = _split3(b)
    return _dot(sel, hi) + _dot(sel, mid) + _dot(sel, lo)


def _dot_sel_rhs(a, sel):
    sel = sel.astype(BF16)
    hi, mid, lo = _split3(a)
    return _dot(hi, sel) + _dot(mid, sel) + _dot(lo, sel)


def _rmsnorm(x, g):
    ms = jnp.mean(x * x, axis=-1, keepdims=True)
    return x * lax.rsqrt(ms + RMS_EPS) * g


def _gelu_tanh(x):
    c = 0.7978845608028654
    half = 0.5 * x
    return half + half * jnp.tanh(x * (c + (0.044715 * c) * (x * x)))


def _silu(x):
    half = 0.5 * x
    return half + half * jnp.tanh(half)


def _softplus(x):
    return jnp.maximum(x, 0.0) + jnp.log1p(jnp.exp(-jnp.abs(x)))


def _mixer0_kernel(x_ref, g_ref, w_in_ref, w_out_ref, lng_ref, lnb_ref, ws_ref, bexp_ref,
                   convw_ref, convb_ref, dtb_ref, alog_ref, dexp_ref, ng_ref, hexp_ref,
                   out_ref, carry_s, state_s, mixa_s, prev_x_s, prev_y_s, prev_z_s,
                   *, tile, tiles_per_seq, d_model, d_inner):
    n_chunks = tile // CHUNK
    d_bc = SSM_GROUPS * SSM_STATE
    step = pl.program_id(0)

    @pl.when(step == 0)
    def _():
        mixa_s[...] = jnp.zeros_like(mixa_s)
        prev_x_s[...] = jnp.zeros_like(prev_x_s)
        prev_y_s[...] = jnp.zeros_like(prev_y_s)
        prev_z_s[...] = jnp.zeros_like(prev_z_s)

    @pl.when(step % tiles_per_seq == 0)
    def _():
        state_s[...] = jnp.zeros_like(state_s)
        carry_s[...] = jnp.zeros_like(carry_s)

    row = lax.broadcasted_iota(jnp.int32, (CHUNK, CHUNK), 0)
    col = lax.broadcasted_iota(jnp.int32, (CHUNK, CHUNK), 1)
    causal = row >= col
    o_u, o_v, o_z, o_x = 0, d_model, 2 * d_model, 2 * d_model + d_inner
    o_dt = o_x + d_inner + 2 * d_bc
    conv_dim = d_inner + 2 * d_bc
    n_slabs_d = d_model // SLAB
    gw = d_inner // SSM_GROUPS


    mixa_prev = mixa_s[...]
    out_acc = [_dot(mixa_prev, w_out_ref[0:d_model, n * SLAB:(n + 1) * SLAB])
               for n in range(n_slabs_d)]
    x = x_ref[0]
    y = _rmsnorm(x, g_ref[...]).astype(BF16)

    def in_proj(lo, width=SLAB):
        return _dot(y, w_in_ref[:, lo:lo + width])

    def conv_silu(raw, lo):
        cols = slice(lo, lo + SLAB)
        head16 = jnp.concatenate([carry_s[:, cols], raw[0:CONV_CARRY, :]], axis=0)
        w_last = convw_ref[SSM_CONV - 1:SSM_CONV, cols]
        conv = convb_ref[:, cols] + w_last * raw
        conv8 = convb_ref[:, cols] + w_last * raw[0:CONV_CARRY, :]
        for k in range(1, SSM_CONV):
            w_k = convw_ref[SSM_CONV - 1 - k:SSM_CONV - k, cols]
            conv = conv + w_k * pltpu.roll(raw, k, axis=0)
            conv8 = conv8 + w_k * pltpu.roll(head16, k, axis=0)[CONV_CARRY:2 * CONV_CARRY, :]
        carry_s[:, cols] = raw[tile - CONV_CARRY:tile, :]
        return _silu(jnp.concatenate([conv8, conv[CONV_CARRY:, :]], axis=0))

    xbc_raw = [in_proj(o_x)]
    yv = prev_y_s[...] * _silu(prev_z_s[...])
    xbc_raw.append(in_proj(o_x + SLAB))
    b_out = jnp.concatenate(
        [_rmsnorm(yv[:, g * gw:(g + 1) * gw], ng_ref[:, g * gw:(g + 1) * gw]).astype(BF16)
         for g in range(SSM_GROUPS)], axis=1)
    xbc_act = []
    for n in range(2, conv_dim // SLAB):
        xbc_raw.append(in_proj(o_x + n * SLAB))
        xbc_act.append(conv_silu(xbc_raw[n - 2], (n - 2) * SLAB))
    dt_raw = in_proj(o_dt, LANES)
    u_raw = [in_proj(o_u)]
    xbc_act.append(conv_silu(xbc_raw[-2], conv_dim - 2 * SLAB))
    u_raw.append(in_proj(o_u + SLAB))
    xbc_act.append(conv_silu(xbc_raw[-1], conv_dim - SLAB))
    xbc = jnp.concatenate(xbc_act, axis=1)

    v_raw = [in_proj(o_v)]
    u = [_gelu_tanh(u_raw[0])]
    dt = _softplus(dt_raw + dtb_ref[...])
    a = dt * (-jnp.exp(alog_ref[...]))
    v_raw.append(in_proj(o_v + SLAB))
    u.append(_gelu_tanh(u_raw[1]))
    u = jnp.concatenate(u, axis=1)

    tril = jnp.where(causal, 1.0, 0.0).astype(F32)
    acums = [_dot_sel_lhs(tril, a[c * CHUNK:(c + 1) * CHUNK, :]) for c in range(n_chunks)]

    v = []
    for n in range(n_slabs_d):
        out_acc[n] = out_acc[n] + _dot(b_out, w_out_ref[d_model:, n * SLAB:(n + 1) * SLAB])
        v.append(_gelu_tanh(v_raw[n]))
    out_ref[0] = prev_x_s[...] + jnp.concatenate(out_acc, axis=1)
    v = jnp.concatenate(v, axis=1)

    chunk_terms = []
    for c in range(n_chunks):
        acum = acums[c]
        dtc = dt[c * CHUNK:(c + 1) * CHUNK, :]
        last = acum[CHUNK - 1:CHUNK, :]
        te_t = (jnp.exp(last - acum) * dtc).T
        decay8 = _dot_sel_rhs(jnp.exp(acum[CHUNK - 8:CHUNK, :]), hexp_ref[...])
        chunk_terms.append((acum, acum.T, dtc.T, te_t, decay8[7:8, :]))

    z_raw = [in_proj(o_z)]
    mu = jnp.mean(v, axis=-1, keepdims=True)
    vc = v - mu
    var = jnp.mean(vc * vc, axis=-1, keepdims=True)
    z_raw.append(in_proj(o_z + SLAB))
    vn = (vc * lax.rsqrt(var + LN_EPS) * lng_ref[...] + lnb_ref[...]).astype(BF16)
    prev_z_s[...] = jnp.concatenate(z_raw, axis=1)
    prev_x_s[...] = x

    xs = xbc[:, 0:d_inner]
    lo_full = lax.broadcasted_iota(jnp.int32, (1, d_inner), 1) % LANES < HALF
    x_lo = jnp.where(lo_full, xs, 0.0).astype(BF16)
    x_hi = jnp.where(lo_full, 0.0, xs).astype(BF16)
    bm = xbc[:, d_inner:d_inner + d_bc]
    cm = xbc[:, d_inner + d_bc:d_inner + 2 * d_bc]
    bm_bf = bm.astype(BF16)
    cm_bf = cm.astype(BF16)
    pairs_per_group = SSM_HEADS // SSM_GROUPS // 2
    state = state_s[...]
    y_chunks = []
    for c in range(n_chunks):
        cs = slice(c * CHUNK, (c + 1) * CHUNK)
        acum, acum_t, dt_t, te_t, chunk_decay = chunk_terms[c]

        st_lo = jnp.where(lo_full, state, 0.0).astype(BF16)
        st_hi = jnp.where(lo_full, 0.0, state).astype(BF16)

        y_pairs = []
        s_pairs = []
        for j in range(SSM_HEADS // 2):
            ps = slice(j * LANES, (j + 1) * LANES)
            if j % pairs_per_group == 0:
                g = j // pairs_per_group
                ns = slice(g * SSM_STATE, (g + 1) * SSM_STATE)
                bg = bm[cs, ns]
                cg = cm[cs, ns]
                cb = _dot_nt(cm_bf[cs, ns], bm_bf[cs, ns])
                bg_t = bg.T
            y_acc = None
            s_acc = None
            for h, x_sel, st_sel in ((2 * j, x_lo, st_lo), (2 * j + 1, x_hi, st_hi)):
                a_col = jnp.broadcast_to(acum[:, h:h + 1], (CHUNK, CHUNK))
                seg = a_col - acum_t[h:h + 1, :]
                decay = jnp.exp(jnp.where(causal, seg, -jnp.inf))
                w_intra = (cb * decay * dt_t[h:h + 1, :]).astype(BF16)
                c_in = (cg * jnp.exp(a_col)).astype(BF16)
                xh = x_sel[cs, ps]
                lhs = jnp.concatenate([w_intra, c_in], axis=1)
                rhs = jnp.concatenate([xh, st_sel[:, ps]], axis=0)
                yh = _dot(lhs, rhs)
                sh = _dot((bg_t * te_t[h:h + 1, :]).astype(BF16), xh)
                y_acc = yh if y_acc is None else y_acc + yh
                s_acc = sh if s_acc is None else s_acc + sh
            y_pairs.append(y_acc)
            s_pairs.append(s_acc)
        y_chunks.append(jnp.concatenate(y_pairs, axis=1))
        state = state * chunk_decay + jnp.concatenate(s_pairs, axis=1)
    state_s[...] = state

    prev_y_s[...] = jnp.concatenate(y_chunks, axis=0) + xs * dexp_ref[...]

    for g in range(GM_GROUPS):
        gs = slice(g * CHUNK, (g + 1) * CHUNK)
        w_g = jnp.where(causal, ws_ref[g], 0.0).astype(BF16)
        for c in range(n_chunks):
            cs = slice(c * CHUNK, (c + 1) * CHUNK)
            mixed = _dot(w_g, vn[cs, gs]) + bexp_ref[:, gs]
            mixa_s[cs, gs] = (u[cs, gs] * mixed).astype(BF16)


def _mixer0(h, g, w_in, w_out, ln_g, ln_b, w_s, b_s, conv_w, conv_b, dt_bias, a_log, d_skip,
            norm_g, *, tile=512):
    bsz, seq, d_model = h.shape
    d_inner = d_model
    d_bc = SSM_GROUPS * SSM_STATE
    conv_dim = d_inner + 2 * d_bc
    n_main = 3 * d_model + conv_dim
    assert w_in.shape == (d_model, n_main + SSM_HEADS)
    assert seq % tile == 0 and tile % CHUNK == 0 and d_inner == SSM_HEADS * HEADDIM

    pad = LANES - SSM_HEADS
    w_in_p = jnp.concatenate([w_in.astype(BF16), jnp.zeros((d_model, pad), BF16)], axis=1)
    row = lambda v: v.reshape(1, -1).astype(F32)
    pad_row = lambda v: jnp.pad(v.astype(F32), (0, pad)).reshape(1, LANES)
    bexp = jnp.repeat(b_s.T.astype(F32), CHUNK, axis=1)
    dexp = jnp.repeat(d_skip.astype(F32), HEADDIM).reshape(1, d_inner)
    hexp = (jnp.arange(LANES)[:, None] == (jnp.arange(d_inner)[None, :] // HEADDIM)).astype(F32)

    tiles_per_seq = seq // tile
    n_tiles = bsz * tiles_per_seq

    def tile_index(i):
        return i // tiles_per_seq, i % tiles_per_seq, 0

    const = lambda shape: pl.BlockSpec(shape, lambda i: (0,) * len(shape))
    kern = functools.partial(_mixer0_kernel, tile=tile, tiles_per_seq=tiles_per_seq,
                             d_model=d_model, d_inner=d_inner)
    return pl.pallas_call(
        kern,
        grid=(n_tiles + 1,),
        in_specs=[
            pl.BlockSpec((1, tile, d_model), lambda i: tile_index(jnp.minimum(i, n_tiles - 1))),
            const((1, d_model)),
            pl.BlockSpec(w_in_p.shape, lambda i: (0, 0), pipeline_mode=pl.Buffered(1)),
            pl.BlockSpec(w_out.shape, lambda i: (0, 0), pipeline_mode=pl.Buffered(1)),
            const((1, d_model)), const((1, d_model)),
            const(w_s.shape),
            const(bexp.shape),
            const(conv_w.shape), const((1, conv_dim)),
            const((1, LANES)), const((1, LANES)),
            const((1, d_inner)), const((1, d_inner)),
            const(hexp.shape),
        ],
        out_specs=pl.BlockSpec((1, tile, d_model), lambda i: tile_index(jnp.maximum(i - 1, 0))),
        out_shape=jax.ShapeDtypeStruct(h.shape, F32),
        scratch_shapes=[
            pltpu.VMEM((CONV_CARRY, conv_dim), F32),
            pltpu.VMEM((SSM_STATE, d_inner), F32),
            pltpu.VMEM((tile, d_model), BF16),
            pltpu.VMEM((tile, d_model), F32),
            pltpu.VMEM((tile, d_inner), F32),
            pltpu.VMEM((tile, d_inner), F32),
        ],
        compiler_params=pltpu.CompilerParams(
            dimension_semantics=("arbitrary",),
            vmem_limit_bytes=VMEM_LIMIT),
        name="mixer0",
    )(h, row(g), w_in_p, w_out.astype(BF16), row(ln_g), row(ln_b), w_s.astype(F32), bexp,
      conv_w.astype(F32), row(conv_b), pad_row(dt_bias), pad_row(a_log), dexp, row(norm_g),
      hexp)


def _mlp_kernel(x_ref, g_ref, w_up_ref, w_down_ref, gf_ref, out_ref, *, ff_chunk, final_norm):
    x = x_ref[...]
    y = _rmsnorm(x, g_ref[...]).astype(BF16)
    acc = x
    d_ff = w_up_ref.shape[1]
    for c in range(d_ff // ff_chunk):
        cs = slice(c * ff_chunk, (c + 1) * ff_chunk)
        hid = jnp.maximum(_dot(y, w_up_ref[:, cs]), 0.0)
        acc = acc + _dot((hid * hid).astype(BF16), w_down_ref[cs, :])
    if final_norm:
        acc = _rmsnorm(acc, gf_ref[...])
    out_ref[...] = acc


def _mlp(h, g, w_up, w_down, g_final, *, layer, final_norm, tile=1024, ff_chunk=1024):
    bsz, seq, d_model = h.shape
    n = bsz * seq
    d_ff = w_up.shape[2]
    assert n % tile == 0 and d_ff % ff_chunk == 0
    x2 = h.reshape(n, d_model)
    const = lambda shape: pl.BlockSpec(shape, lambda i: (0,) * len(shape))
    of_layer = lambda shape: pl.BlockSpec((None,) + shape, lambda i: (layer, 0, 0),
                                          pipeline_mode=pl.Buffered(1))
    out = pl.pallas_call(
        functools.partial(_mlp_kernel, ff_chunk=ff_chunk, final_norm=final_norm),
        grid=(n // tile,),
        in_specs=[
            pl.BlockSpec((tile, d_model), lambda i: (i, 0)),
            const((1, d_model)),
            of_layer((d_model, d_ff)),
            of_layer((d_ff, d_model)),
            const((1, d_model)),
        ],
        out_specs=pl.BlockSpec((tile, d_model), lambda i: (i, 0)),
        out_shape=jax.ShapeDtypeStruct((n, d_model), F32),
        compiler_params=pltpu.CompilerParams(
            dimension_semantics=("arbitrary",), vmem_limit_bytes=VMEM_LIMIT),
        name="mlp_final" if final_norm else "mlp",
    )(x2, g.reshape(1, -1).astype(F32), w_up, w_down, g_final.reshape(1, -1).astype(F32))
    return out.reshape(bsz, seq, d_model)


def _attn_kernel(sinks_ref, x_ref, g_ref, wq_ref, bq_ref, wk_ref, bk_ref, wvt_ref, bvt_ref,
                 wo_ref, bo_ref, out_ref, q_s, k_s, vt_s, att_s, *, tile):
    n_blocks = tile // CHUNK
    grp_rows = PAIRS_PER_KV * CHUNK
    t_idx = pl.program_id(1)

    @pl.when(t_idx == 0)
    def _():
        k_s[:, 0:CHUNK, :] = jnp.zeros((4, CHUNK, LANES), BF16)
        vt_s[:, 0] = jnp.zeros((4, LANES, CHUNK), BF16)

    @pl.when(t_idx != 0)
    def _():
        k_s[:, 0:CHUNK, :] = k_s[:, tile:tile + CHUNK, :]
        vt_s[:, 0] = vt_s[:, n_blocks]

    n_var = 2 * ATTN_KV_HEADS
    part_rows = tile // ATTN_PARTS
    blocks_per_part = n_blocks // ATTN_PARTS
    q_slab = 2 * LANES
    lo_lane = lax.broadcasted_iota(jnp.int32, (1, LANES), 1) < HALF
    lo_row = lax.broadcasted_iota(jnp.int32, (LANES, 1), 0) < HALF
    bvt = jnp.concatenate([bvt_ref[...]] * blocks_per_part, axis=1)

    def rows_of(part):
        return slice(part * part_rows, (part + 1) * part_rows)

    y_parts = [_rmsnorm(x_ref[0, rows_of(part), :], g_ref[...]).astype(BF16)
               for part in range(ATTN_PARTS)]

    def q_piece(part, n):
        cols = slice(n * q_slab, (n + 1) * q_slab)
        q = ((_dot(y_parts[part], wq_ref[:, cols]) + bq_ref[:, cols])
             * (HEADDIM ** -0.5 * LOG2E)).astype(BF16)
        for b in range(blocks_per_part):
            for jj in range(q_slab // LANES):
                j = n * (q_slab // LANES) + jj
                q_s[part * blocks_per_part + b, j * CHUNK:(j + 1) * CHUNK, :] = (
                    q[b * CHUNK:(b + 1) * CHUNK, jj * LANES:(jj + 1) * LANES])

    def k_piece(part):
        k = _dot(y_parts[part], wk_ref[...]) + bk_ref[...]
        k_swp = pltpu.roll(k, HALF, axis=1)
        k_variants = (jnp.where(lo_lane, k, 0.0), jnp.where(lo_lane, 0.0, k_swp),
                      jnp.where(lo_lane, k_swp, 0.0), jnp.where(lo_lane, 0.0, k))
        lo = CHUNK + part * part_rows
        for g, val in enumerate(k_variants):
            k_s[g, lo:lo + part_rows, :] = val.astype(BF16)

    def v_piece(part):
        vt = _dot_nt(wvt_ref[...], y_parts[part]) + bvt
        vt_swp = pltpu.roll(vt, HALF, axis=0)
        v_variants = (jnp.where(lo_row, vt, 0.0), jnp.where(lo_row, 0.0, vt_swp),
                      jnp.where(lo_row, vt_swp, 0.0), jnp.where(lo_row, 0.0, vt))
        for g, val in enumerate(v_variants):
            for b in range(blocks_per_part):
                vt_s[g, 1 + part * blocks_per_part + b] = (
                    val[:, b * CHUNK:(b + 1) * CHUNK].astype(BF16))

    def o_piece(part, n):
        cols = slice(n * q_slab, (n + 1) * q_slab)
        rows = rows_of(part)
        out_ref[0, rows, cols] = (x_ref[0, rows, cols] + _dot(att_s[rows, :], wo_ref[:, cols])
                                  + bo_ref[:, cols])

    n_q_pieces = wq_ref.shape[1] // q_slab
    n_o_pieces = wo_ref.shape[1] // q_slab
    for n in range(n_q_pieces):
        q_piece(0, n)
    k_piece(0)
    v_piece(0)
    fillers = []
    for part in range(ATTN_PARTS):
        todo = []
        if part + 1 < ATTN_PARTS:
            todo += [functools.partial(q_piece, part + 1, n) for n in range(n_q_pieces)]
            todo += [functools.partial(k_piece, part + 1), functools.partial(v_piece, part + 1)]
        if part > 0:
            todo += [functools.partial(o_piece, part - 1, n) for n in range(n_o_pieces)]
        fillers.append(todo)
    slots_per_part = blocks_per_part * n_var

    si = lax.broadcasted_iota(jnp.int32, (2 * CHUNK, CHUNK), 0)
    qi = lax.broadcasted_iota(jnp.int32, (2 * CHUNK, CHUNK), 1)
    valid = (si > qi) & (si <= qi + CHUNK)
    neg_inf = jnp.float32(-jnp.inf)
    bias = jnp.where(valid, 0.0, neg_inf)
    bias_first = jnp.where(valid & (si >= CHUNK), 0.0, neg_inf)

    def scores_t(i, g):
        kv = g // 2
        return _dot_nt(k_s[g, i * CHUNK:(i + 2) * CHUNK, :],
                       q_s[i, kv * grp_rows:(kv + 1) * grp_rows, :])

    st_next = scores_t(0, 0)
    for i in range(n_blocks):
        blk_bias = jnp.where(t_idx == 0, bias_first, bias) if i == 0 else bias
        todo = fillers[i // blocks_per_part]
        if i % blocks_per_part == 0:
            slots_left = slots_per_part
        for kv in range(ATTN_KV_HEADS):
            o_acc = None
            for par in range(2):
                g = 2 * kv + par
                st_all = st_next
                if g + 1 < n_var:
                    st_next = scores_t(i, g + 1)
                elif i + 1 < n_blocks:
                    st_next = scores_t(i + 1, 0)
                p_cols = []
                inv_cols = []
                for jj in range(PAIRS_PER_KV):
                    head = 2 * (kv * PAIRS_PER_KV + jj) + par
                    s = st_all[:, jj * CHUNK:(jj + 1) * CHUNK] + blk_bias
                    sink = sinks_ref[head] * LOG2E
                    m = jnp.maximum(jnp.max(s, axis=0, keepdims=True), sink)
                    p = jnp.exp2(s - m)
                    denom = jnp.sum(p, axis=0, keepdims=True) + jnp.exp2(sink - m)
                    p_cols.append(p.astype(BF16))
                    inv_cols.append(1.0 / denom)
                v_band = jnp.concatenate([vt_s[g, i], vt_s[g, i + 1]], axis=1)
                ot = _dot(v_band, jnp.concatenate(p_cols, axis=1))
                ot = ot * jnp.concatenate(inv_cols, axis=1)
                o_acc = ot if o_acc is None else o_acc + ot
                for _ in range(-(-len(todo) // slots_left)):
                    todo.pop(0)()
                slots_left -= 1
            for jj in range(PAIRS_PER_KV):
                j = kv * PAIRS_PER_KV + jj
                att_s[i * CHUNK:(i + 1) * CHUNK, j * LANES:(j + 1) * LANES] = (
                    o_acc[:, jj * CHUNK:(jj + 1) * CHUNK].T.astype(BF16))

    assert not any(fillers)
    for n in range(n_o_pieces):
        o_piece(ATTN_PARTS - 1, n)


def _attn(h, g, w_qkv, b_qkv, w_o, b_o, sinks, *, tile=1024):
    bsz, seq, d_model = h.shape
    d_q = ATTN_HEADS * HEADDIM
    d_kv = ATTN_KV_HEADS * HEADDIM
    assert w_qkv.shape == (d_model, d_q + 2 * d_kv) and d_kv == LANES
    assert seq % tile == 0 and tile % CHUNK == 0
    n_blocks = tile // CHUNK
    row = lambda v: v.reshape(1, -1).astype(F32)
    w_q, w_k, w_v = w_qkv[:, :d_q], w_qkv[:, d_q:d_q + d_kv], w_qkv[:, d_q + d_kv:]
    b_q, b_k, b_v = b_qkv[:d_q], b_qkv[d_q:d_q + d_kv], b_qkv[d_q + d_kv:]
    b_vt = jnp.broadcast_to(b_v.astype(F32)[:, None], (d_kv, CHUNK))
    const = lambda shape: pl.BlockSpec(shape, lambda b, t, s: (0,) * len(shape))
    grid_spec = pltpu.PrefetchScalarGridSpec(
        num_scalar_prefetch=1,
        grid=(bsz, seq // tile),
        in_specs=[
            pl.BlockSpec((1, tile, d_model), lambda b, t, s: (b, t, 0)),
            const((1, d_model)),
            const((d_model, d_q)), const((1, d_q)),
            const((d_model, d_kv)), const((1, d_kv)),
            const((d_kv, d_model)), const((d_kv, CHUNK)),
            const((d_q, d_model)), const((1, d_model)),
        ],
        out_specs=pl.BlockSpec((1, tile, d_model), lambda b, t, s: (b, t, 0)),
        scratch_shapes=[
            pltpu.VMEM((n_blocks, ATTN_PAIRS * CHUNK, LANES), BF16),
            pltpu.VMEM((4, tile + CHUNK, LANES), BF16),
            pltpu.VMEM((4, n_blocks + 1, LANES, CHUNK), BF16),
            pltpu.VMEM((tile, d_q), BF16),
        ],
    )
    return pl.pallas_call(
        functools.partial(_attn_kernel, tile=tile),
        grid_spec=grid_spec,
        out_shape=jax.ShapeDtypeStruct(h.shape, F32),
        compiler_params=pltpu.CompilerParams(
            dimension_semantics=("arbitrary", "arbitrary"),
            vmem_limit_bytes=VMEM_LIMIT),
        name="attn",
    )(sinks.astype(F32), h, row(g), w_q.astype(BF16), row(b_q), w_k.astype(BF16), row(b_k),
      w_v.T.astype(BF16), b_vt, w_o.astype(BF16), row(b_o))


def kernel(x, norm_mix_g, norm_mlp_g, final_norm_g, w_in_even, w_out_even, gm_ln_g, gm_ln_b,
           gm_w_s, gm_b_s, ssm_conv_w, ssm_conv_b, ssm_dt_bias, ssm_a_log, ssm_d, ssm_norm_g,
           w_qkv, b_qkv, w_o, b_o, attn_sinks, w_up, w_down):
    depth = norm_mix_g.shape[0]
    w_up_bf = w_up.astype(BF16)
    w_down_bf = w_down.astype(BF16)
    h = x
    for i in range(depth):
        j = i // 2
        if i % 2 == 0:
            h = _mixer0(h, norm_mix_g[i], w_in_even[j], w_out_even[j], gm_ln_g[j], gm_ln_b[j],
                        gm_w_s[j], gm_b_s[j], ssm_conv_w[j], ssm_conv_b[j], ssm_dt_bias[j],
                        ssm_a_log[j], ssm_d[j], ssm_norm_g[j])
        else:
            h = _attn(h, norm_mix_g[i], w_qkv[j], b_qkv[j], w_o[j], b_o[j], attn_sinks[j])
        h = _mlp(h, norm_mlp_g[i], w_up_bf, w_down_bf, final_norm_g, layer=i,
                 final_norm=(i == depth - 1))
    return h
```

```python
import functools

import jax
import jax.numpy as jnp
from jax import lax
from jax.experimental import pallas as pl
from jax.experimental.pallas import tpu as pltpu

F32 = jnp.float32
BF16 = jnp.bfloat16

RMS_EPS = 1e-5
LN_EPS = 1e-5

LANES = 128
SUBLANES = 8
CHUNK = 128
HEADDIM = 64
HALF = LANES // 2
LOG2E = 1.4426950408889634

GM_GROUPS = 8
SSM_HEADS = 16
SSM_GROUPS = 4
SSM_STATE = 128
SSM_CONV = 4
SLAB = 512
CONV_CARRY = SUBLANES

ATTN_HEADS = 16
ATTN_KV_HEADS = 2
ATTN_PAIRS = ATTN_HEADS // 2
PAIRS_PER_KV = ATTN_PAIRS // ATTN_KV_HEADS
ATTN_PARTS = 4

VMEM_LIMIT = 56 * 1024 * 1024


def _dot(a, b):
    return jnp.dot(a, b, preferred_element_type=F32)


def _dot_nt(a, b):
    return lax.dot_general(a, b, (((1,), (1,)), ((), ())), preferred_element_type=F32)


def _split3(x):
    hi = x.astype(BF16)
    r1 = x - hi.astype(F32)
    mid = r1.astype(BF16)
    lo = (r1 - mid.astype(F32)).astype(BF16)
    return hi, mid, lo


def _dot_sel_lhs(sel, b):
    sel = sel.astype(BF16)
    hi, mid, lo = _split3(b)
    return _dot(sel, hi) + _dot(sel, mid) + _dot(sel, lo)


def _dot_sel_rhs(a, sel):
    sel = sel.astype(BF16)
    hi, mid, lo = _split3(a)
    return _dot(hi, sel) + _dot(mid, sel) + _dot(lo, sel)


def _rmsnorm(x, g):
    ms = jnp.mean(x * x, axis=-1, keepdims=True)
    return x * lax.rsqrt(ms + RMS_EPS) * g


def _gelu_tanh(x):
    c = 0.7978845608028654
    half = 0.5 * x
    return half + half * jnp.tanh(x * (c + (0.044715 * c) * (x * x)))


def _silu(x):
    half = 0.5 * x
    return half + half * jnp.tanh(half)


def _softplus(x):
    return jnp.maximum(x, 0.0) + jnp.log1p(jnp.exp(-jnp.abs(x)))


def _mixer0_kernel(x_ref, g_ref, w_in_ref, w_out_ref, lng_ref, lnb_ref, ws_ref, bexp_ref,
                   convw_ref, convb_ref, dtb_ref, alog_ref, dexp_ref, ng_ref, hexp_ref,
                   out_ref, carry_s, state_s, mixa_s, prev_x_s, prev_y_s, prev_z_s,
                   *, tile, tiles_per_seq, d_model, d_inner):
    n_chunks = tile // CHUNK
    d_bc = SSM_GROUPS * SSM_STATE
    step = pl.program_id(0)

    @pl.when(step == 0)
    def _():
        mixa_s[...] = jnp.zeros_like(mixa_s)
        prev_x_s[...] = jnp.zeros_like(prev_x_s)
        prev_y_s[...] = jnp.zeros_like(prev_y_s)
        prev_z_s[...] = jnp.zeros_like(prev_z_s)

    @pl.when(step % tiles_per_seq == 0)
    def _():
        state_s[...] = jnp.zeros_like(state_s)
        carry_s[...] = jnp.zeros_like(carry_s)

    row = lax.broadcasted_iota(jnp.int32, (CHUNK, CHUNK), 0)
    col = lax.broadcasted_iota(jnp.int32, (CHUNK, CHUNK), 1)
    causal = row >= col
    o_u, o_v, o_z, o_x = 0, d_model, 2 * d_model, 2 * d_model + d_inner
    o_dt = o_x + d_inner + 2 * d_bc
    conv_dim = d_inner + 2 * d_bc
    n_slabs_d = d_model // SLAB
    gw = d_inner // SSM_GROUPS


    mixa_prev = mixa_s[...]
    out_acc = [_dot(mixa_prev, w_out_ref[0:d_model, n * SLAB:(n + 1) * SLAB])
               for n in range(n_slabs_d)]
    x = x_ref[0]
    y = _rmsnorm(x, g_ref[...]).astype(BF16)

    def in_proj(lo, width=SLAB):
        return _dot(y, w_in_ref[:, lo:lo + width])

    def conv_silu(raw, lo):
        cols = slice(lo, lo + SLAB)
        head16 = jnp.concatenate([carry_s[:, cols], raw[0:CONV_CARRY, :]], axis=0)
        w_last = convw_ref[SSM_CONV - 1:SSM_CONV, cols]
        conv = convb_ref[:, cols] + w_last * raw
        conv8 = convb_ref[:, cols] + w_last * raw[0:CONV_CARRY, :]
        for k in range(1, SSM_CONV):
            w_k = convw_ref[SSM_CONV - 1 - k:SSM_CONV - k, cols]
            conv = conv + w_k * pltpu.roll(raw, k, axis=0)
            conv8 = conv8 + w_k * pltpu.roll(head16, k, axis=0)[CONV_CARRY:2 * CONV_CARRY, :]
        carry_s[:, cols] = raw[tile - CONV_CARRY:tile, :]
        return _silu(jnp.concatenate([conv8, conv[CONV_CARRY:, :]], axis=0))

    xbc_raw = [in_proj(o_x)]
    yv = prev_y_s[...] * _silu(prev_z_s[...])
    xbc_raw.append(in_proj(o_x + SLAB))
    b_out = jnp.concatenate(
        [_rmsnorm(yv[:, g * gw:(g + 1) * gw], ng_ref[:, g * gw:(g + 1) * gw]).astype(BF16)
         for g in range(SSM_GROUPS)], axis=1)
    xbc_act = []
    for n in range(2, conv_dim // SLAB):
        xbc_raw.append(in_proj(o_x + n * SLAB))
        xbc_act.append(conv_silu(xbc_raw[n - 2], (n - 2) * SLAB))
    dt_raw = in_proj(o_dt, LANES)
    u_raw = [in_proj(o_u)]
    xbc_act.append(conv_silu(xbc_raw[-2], conv_dim - 2 * SLAB))
    u_raw.append(in_proj(o_u + SLAB))
    xbc_act.append(conv_silu(xbc_raw[-1], conv_dim - SLAB))
    xbc = jnp.concatenate(xbc_act, axis=1)

    v_raw = [in_proj(o_v)]
    u = [_gelu_tanh(u_raw[0])]
    dt = _softplus(dt_raw + dtb_ref[...])
    a = dt * (-jnp.exp(alog_ref[...]))
    v_raw.append(in_proj(o_v + SLAB))
    u.append(_gelu_tanh(u_raw[1]))
    u = jnp.concatenate(u, axis=1)

    tril = jnp.where(causal, 1.0, 0.0).astype(F32)
    acums = [_dot_sel_lhs(tril, a[c * CHUNK:(c + 1) * CHUNK, :]) for c in range(n_chunks)]

    v = []
    for n in range(n_slabs_d):
        out_acc[n] = out_acc[n] + _dot(b_out, w_out_ref[d_model:, n * SLAB:(n + 1) * SLAB])
        v.append(_gelu_tanh(v_raw[n]))
    out_ref[0] = prev_x_s[...] + jnp.concatenate(out_acc, axis=1)
    v = jnp.concatenate(v, axis=1)

    chunk_terms = []
    for c in range(n_chunks):
        acum = acums[c]
        dtc = dt[c * CHUNK:(c + 1) * CHUNK, :]
        last = acum[CHUNK - 1:CHUNK, :]
        te_t = (jnp.exp(last - acum) * dtc).T
        decay = _dot_sel_rhs(jnp.exp(acum[CHUNK - SUBLANES:CHUNK, :]), hexp_ref[...])
        chunk_terms.append((acum, acum.T, dtc.T, te_t, decay[SUBLANES - 1:SUBLANES, :]))

    z_raw = [in_proj(o_z)]
    mu = jnp.mean(v, axis=-1, keepdims=True)
    vc = v - mu
    var = jnp.mean(vc * vc, axis=-1, keepdims=True)
    z_raw.append(in_proj(o_z + SLAB))
    vn = (vc * lax.rsqrt(var + LN_EPS) * lng_ref[...] + lnb_ref[...]).astype(BF16)
    prev_z_s[...] = jnp.concatenate(z_raw, axis=1)
    prev_x_s[...] = x

    xs = xbc[:, 0:d_inner]
    lo_full = lax.broadcasted_iota(jnp.int32, (1, d_inner), 1) % LANES < HALF
    x_lo = jnp.where(lo_full, xs, 0.0).astype(BF16)
    x_hi = jnp.where(lo_full, 0.0, xs).astype(BF16)
    bm = xbc[:, d_inner:d_inner + d_bc]
    cm = xbc[:, d_inner + d_bc:d_inner + 2 * d_bc]
    bm_bf = bm.astype(BF16)
    cm_bf = cm.astype(BF16)
    pairs_per_group = SSM_HEADS // SSM_GROUPS // 2
    state = state_s[...]
    y_chunks = []
    for c in range(n_chunks):
        cs = slice(c * CHUNK, (c + 1) * CHUNK)
        acum, acum_t, dt_t, te_t, chunk_decay = chunk_terms[c]

        st_lo = jnp.where(lo_full, state, 0.0).astype(BF16)
        st_hi = jnp.where(lo_full, 0.0, state).astype(BF16)

        y_pairs = []
        s_pairs = []
        for j in range(SSM_HEADS // 2):
            ps = slice(j * LANES, (j + 1) * LANES)
            if j % pairs_per_group == 0:
                g = j // pairs_per_group
                ns = slice(g * SSM_STATE, (g + 1) * SSM_STATE)
                bg = bm[cs, ns]
                cg = cm[cs, ns]
                cb = _dot_nt(cm_bf[cs, ns], bm_bf[cs, ns])
                bg_t = bg.T
            y_acc = None
            s_acc = None
            for h, x_sel, st_sel in ((2 * j, x_lo, st_lo), (2 * j + 1, x_hi, st_hi)):
                a_col = jnp.broadcast_to(acum[:, h:h + 1], (CHUNK, CHUNK))
                seg = a_col - acum_t[h:h + 1, :]
                decay = jnp.exp(jnp.where(causal, seg, -jnp.inf))
                w_intra = (cb * decay * dt_t[h:h + 1, :]).astype(BF16)
                c_in = (cg * jnp.exp(a_col)).astype(BF16)
                xh = x_sel[cs, ps]
                lhs = jnp.concatenate([w_intra, c_in], axis=1)
                rhs = jnp.concatenate([xh, st_sel[:, ps]], axis=0)
                yh = _dot(lhs, rhs)
                sh = _dot((bg_t * te_t[h:h + 1, :]).astype(BF16), xh)
                y_acc = yh if y_acc is None else y_acc + yh
                s_acc = sh if s_acc is None else s_acc + sh
            y_pairs.append(y_acc)
            s_pairs.append(s_acc)
        y_chunks.append(jnp.concatenate(y_pairs, axis=1))
        state = state * chunk_decay + jnp.concatenate(s_pairs, axis=1)
    state_s[...] = state

    prev_y_s[...] = jnp.concatenate(y_chunks, axis=0) + xs * dexp_ref[...]

    for g in range(GM_GROUPS):
        gs = slice(g * CHUNK, (g + 1) * CHUNK)
        w_g = jnp.where(causal, ws_ref[g], 0.0).astype(BF16)
        for c in range(n_chunks):
            cs = slice(c * CHUNK, (c + 1) * CHUNK)
            mixed = _dot(w_g, vn[cs, gs]) + bexp_ref[:, gs]
            mixa_s[cs, gs] = (u[cs, gs] * mixed).astype(BF16)


def _mixer0(h, g, w_in, w_out, ln_g, ln_b, w_s, b_s, conv_w, conv_b, dt_bias, a_log, d_skip,
            norm_g, *, tile=512):
    bsz, seq, d_model = h.shape
    d_inner = d_model
    d_bc = SSM_GROUPS * SSM_STATE
    conv_dim = d_inner + 2 * d_bc
    n_main = 3 * d_model + conv_dim
    assert w_in.shape == (d_model, n_main + SSM_HEADS)
    assert seq % tile == 0 and tile % CHUNK == 0 and d_inner == SSM_HEADS * HEADDIM

    pad = LANES - SSM_HEADS
    w_in_p = jnp.concatenate([w_in.astype(BF16), jnp.zeros((d_model, pad), BF16)], axis=1)
    row = lambda v: v.reshape(1, -1).astype(F32)
    pad_row = lambda v: jnp.pad(v.astype(F32), (0, pad)).reshape(1, LANES)
    bexp = jnp.repeat(b_s.T.astype(F32), CHUNK, axis=1)
    dexp = jnp.repeat(d_skip.astype(F32), HEADDIM).reshape(1, d_inner)
    hexp = (jnp.arange(LANES)[:, None] == (jnp.arange(d_inner)[None, :] // HEADDIM)).astype(F32)

    tiles_per_seq = seq // tile
    n_tiles = bsz * tiles_per_seq

    def tile_index(i):
        return i // tiles_per_seq, i % tiles_per_seq, 0

    const = lambda shape: pl.BlockSpec(shape, lambda i: (0,) * len(shape))
    kern = functools.partial(_mixer0_kernel, tile=tile, tiles_per_seq=tiles_per_seq,
                             d_model=d_model, d_inner=d_inner)
    return pl.pallas_call(
        kern,
        grid=(n_tiles + 1,),
        in_specs=[
            pl.BlockSpec((1, tile, d_model), lambda i: tile_index(jnp.minimum(i, n_tiles - 1))),
            const((1, d_model)),
            pl.BlockSpec(w_in_p.shape, lambda i: (0, 0), pipeline_mode=pl.Buffered(1)),
            pl.BlockSpec(w_out.shape, lambda i: (0, 0), pipeline_mode=pl.Buffered(1)),
            const((1, d_model)), const((1, d_model)),
            const(w_s.shape),
            const(bexp.shape),
            const(conv_w.shape), const((1, conv_dim)),
            const((1, LANES)), const((1, LANES)),
            const((1, d_inner)), const((1, d_inner)),
            const(hexp.shape),
        ],
        out_specs=pl.BlockSpec((1, tile, d_model), lambda i: tile_index(jnp.maximum(i - 1, 0))),
        out_shape=jax.ShapeDtypeStruct(h.shape, F32),
        scratch_shapes=[
            pltpu.VMEM((CONV_CARRY, conv_dim), F32),
            pltpu.VMEM((SSM_STATE, d_inner), F32),
            pltpu.VMEM((tile, d_model), BF16),
            pltpu.VMEM((tile, d_model), F32),
            pltpu.VMEM((tile, d_inner), F32),
            pltpu.VMEM((tile, d_inner), F32),
        ],
        compiler_params=pltpu.CompilerParams(
            dimension_semantics=("arbitrary",),
            vmem_limit_bytes=VMEM_LIMIT),
        name="mixer0",
    )(h, row(g), w_in_p, w_out.astype(BF16), row(ln_g), row(ln_b), w_s.astype(F32), bexp,
      conv_w.astype(F32), row(conv_b), pad_row(dt_bias), pad_row(a_log), dexp, row(norm_g),
      hexp)


def _mlp_kernel(x_ref, g_ref, w_up_ref, w_down_ref, gf_ref, out_ref, *, ff_chunk, final_norm):
    x = x_ref[...]
    y = _rmsnorm(x, g_ref[...]).astype(BF16)
    acc = x
    d_ff = w_up_ref.shape[1]
    for c in range(d_ff // ff_chunk):
        cs = slice(c * ff_chunk, (c + 1) * ff_chunk)
        hid = jnp.maximum(_dot(y, w_up_ref[:, cs]), 0.0)
        acc = acc + _dot((hid * hid).astype(BF16), w_down_ref[cs, :])
    if final_norm:
        acc = _rmsnorm(acc, gf_ref[...])
    out_ref[...] = acc


def _mlp(h, g, w_up, w_down, g_final, *, layer, final_norm, tile=1024, ff_chunk=1024):
    bsz, seq, d_model = h.shape
    n = bsz * seq
    d_ff = w_up.shape[2]
    assert n % tile == 0 and d_ff % ff_chunk == 0
    x2 = h.reshape(n, d_model)
    const = lambda shape: pl.BlockSpec(shape, lambda i: (0,) * len(shape))
    of_layer = lambda shape: pl.BlockSpec((None,) + shape, lambda i: (layer, 0, 0),
                                          pipeline_mode=pl.Buffered(1))
    out = pl.pallas_call(
        functools.partial(_mlp_kernel, ff_chunk=ff_chunk, final_norm=final_norm),
        grid=(n // tile,),
        in_specs=[
            pl.BlockSpec((tile, d_model), lambda i: (i, 0)),
            const((1, d_model)),
            of_layer((d_model, d_ff)),
            of_layer((d_ff, d_model)),
            const((1, d_model)),
        ],
        out_specs=pl.BlockSpec((tile, d_model), lambda i: (i, 0)),
        out_shape=jax.ShapeDtypeStruct((n, d_model), F32),
        compiler_params=pltpu.CompilerParams(
            dimension_semantics=("arbitrary",), vmem_limit_bytes=VMEM_LIMIT),
        name="mlp_final" if final_norm else "mlp",
    )(x2, g.reshape(1, -1).astype(F32), w_up, w_down, g_final.reshape(1, -1).astype(F32))
    return out.reshape(bsz, seq, d_model)


def _attn_kernel(sinks_ref, x_ref, g_ref, wq_ref, bq_ref, wk_ref, bk_ref, wvt_ref, bvt_ref,
                 wo_ref, bo_ref, out_ref, q_s, k_s, vt_s, att_s, *, tile):
    n_blocks = tile // CHUNK
    grp_rows = PAIRS_PER_KV * CHUNK
    t_idx = pl.program_id(1)

    @pl.when(t_idx == 0)
    def _():
        k_s[:, 0:CHUNK, :] = jnp.zeros((4, CHUNK, LANES), BF16)
        vt_s[:, 0] = jnp.zeros((ATTN_KV_HEADS, HEADDIM, CHUNK), BF16)

    @pl.when(t_idx != 0)
    def _():
        k_s[:, 0:CHUNK, :] = k_s[:, tile:tile + CHUNK, :]
        vt_s[:, 0] = vt_s[:, n_blocks]

    n_var = 2 * ATTN_KV_HEADS
    part_rows = tile // ATTN_PARTS
    blocks_per_part = n_blocks // ATTN_PARTS
    q_slab = 2 * LANES
    lo_lane = lax.broadcasted_iota(jnp.int32, (1, LANES), 1) < HALF
    bvt = jnp.concatenate([bvt_ref[...]] * blocks_per_part, axis=1)

    def rows_of(part):
        return slice(part * part_rows, (part + 1) * part_rows)

    y_parts = [_rmsnorm(x_ref[0, rows_of(part), :], g_ref[...]).astype(BF16)
               for part in range(ATTN_PARTS)]

    def q_piece(part, n):
        cols = slice(n * q_slab, (n + 1) * q_slab)
        q = ((_dot(y_parts[part], wq_ref[:, cols]) + bq_ref[:, cols])
             * (HEADDIM ** -0.5 * LOG2E)).astype(BF16)
        for b in range(blocks_per_part):
            for jj in range(q_slab // LANES):
                j = n * (q_slab // LANES) + jj
                q_s[part * blocks_per_part + b, j * CHUNK:(j + 1) * CHUNK, :] = (
                    q[b * CHUNK:(b + 1) * CHUNK, jj * LANES:(jj + 1) * LANES])

    def k_piece(part):
        k = _dot(y_parts[part], wk_ref[...]) + bk_ref[...]
        k_swp = pltpu.roll(k, HALF, axis=1)
        k_variants = (jnp.where(lo_lane, k, 0.0), jnp.where(lo_lane, 0.0, k_swp),
                      jnp.where(lo_lane, k_swp, 0.0), jnp.where(lo_lane, 0.0, k))
        lo = CHUNK + part * part_rows
        for g, val in enumerate(k_variants):
            k_s[g, lo:lo + part_rows, :] = val.astype(BF16)

    def v_piece(part):
        vt = _dot_nt(wvt_ref[...], y_parts[part]) + bvt
        for kv in range(ATTN_KV_HEADS):
            for b in range(blocks_per_part):
                vt_s[kv, 1 + part * blocks_per_part + b] = (
                    vt[kv * HEADDIM:(kv + 1) * HEADDIM, b * CHUNK:(b + 1) * CHUNK].astype(BF16))

    def o_piece(part, n):
        cols = slice(n * q_slab, (n + 1) * q_slab)
        rows = rows_of(part)
        out_ref[0, rows, cols] = (x_ref[0, rows, cols] + _dot(att_s[rows, :], wo_ref[:, cols])
                                  + bo_ref[:, cols])

    n_q_pieces = wq_ref.shape[1] // q_slab
    n_o_pieces = wo_ref.shape[1] // q_slab
    for n in range(n_q_pieces):
        q_piece(0, n)
    k_piece(0)
    v_piece(0)
    fillers = []
    for part in range(ATTN_PARTS):
        todo = []
        if part + 1 < ATTN_PARTS:
            todo += [functools.partial(q_piece, part + 1, n) for n in range(n_q_pieces)]
            todo += [functools.partial(k_piece, part + 1), functools.partial(v_piece, part + 1)]
        if part > 0:
            todo += [functools.partial(o_piece, part - 1, n) for n in range(n_o_pieces)]
        fillers.append(todo)
    slots_per_part = blocks_per_part * n_var

    si = lax.broadcasted_iota(jnp.int32, (2 * CHUNK, CHUNK), 0)
    qi = lax.broadcasted_iota(jnp.int32, (2 * CHUNK, CHUNK), 1)
    valid = (si > qi) & (si <= qi + CHUNK)
    neg_inf = jnp.float32(-jnp.inf)
    bias = jnp.where(valid, 0.0, neg_inf)
    bias_first = jnp.where(valid & (si >= CHUNK), 0.0, neg_inf)

    def scores_t(i, g):
        kv = g // 2
        return _dot_nt(k_s[g, i * CHUNK:(i + 2) * CHUNK, :],
                       q_s[i, kv * grp_rows:(kv + 1) * grp_rows, :])

    st_next = scores_t(0, 0)
    for i in range(n_blocks):
        blk_bias = jnp.where(t_idx == 0, bias_first, bias) if i == 0 else bias
        todo = fillers[i // blocks_per_part]
        if i % blocks_per_part == 0:
            slots_left = slots_per_part
        for kv in range(ATTN_KV_HEADS):
            p_cols = []
            inv_cols = []
            for par in range(2):
                g = 2 * kv + par
                st_all = st_next
                if g + 1 < n_var:
                    st_next = scores_t(i, g + 1)
                elif i + 1 < n_blocks:
                    st_next = scores_t(i + 1, 0)
                for jj in range(PAIRS_PER_KV):
                    head = 2 * (kv * PAIRS_PER_KV + jj) + par
                    s = st_all[:, jj * CHUNK:(jj + 1) * CHUNK] + blk_bias
                    sink = sinks_ref[head] * LOG2E
                    m = jnp.maximum(jnp.max(s, axis=0, keepdims=True), sink)
                    p = jnp.exp2(s - m)
                    denom = jnp.sum(p, axis=0, keepdims=True) + jnp.exp2(sink - m)
                    p_cols.append(p.astype(BF16))
                    inv_cols.append(1.0 / denom)
                if par == 1:
                    v_band = jnp.concatenate([vt_s[kv, i], vt_s[kv, i + 1]], axis=1)
                    ot = _dot(v_band, jnp.concatenate(p_cols, axis=1))
                    ot = ot * jnp.concatenate(inv_cols, axis=1)
                for _ in range(-(-len(todo) // slots_left)):
                    todo.pop(0)()
                slots_left -= 1
            for jj in range(PAIRS_PER_KV):
                j = kv * PAIRS_PER_KV + jj
                o_pair = jnp.concatenate(
                    [ot[:, jj * CHUNK:(jj + 1) * CHUNK],
                     ot[:, (PAIRS_PER_KV + jj) * CHUNK:(PAIRS_PER_KV + jj + 1) * CHUNK]], axis=0)
                att_s[i * CHUNK:(i + 1) * CHUNK, j * LANES:(j + 1) * LANES] = (
                    o_pair.T.astype(BF16))

    assert not any(fillers)
    for n in range(n_o_pieces):
        o_piece(ATTN_PARTS - 1, n)


def _attn(h, g, w_qkv, b_qkv, w_o, b_o, sinks, *, tile=1024):
    bsz, seq, d_model = h.shape
    d_q = ATTN_HEADS * HEADDIM
    d_kv = ATTN_KV_HEADS * HEADDIM
    assert w_qkv.shape == (d_model, d_q + 2 * d_kv) and d_kv == LANES
    assert seq % tile == 0 and tile % CHUNK == 0
    n_blocks = tile // CHUNK
    row = lambda v: v.reshape(1, -1).astype(F32)
    w_q, w_k, w_v = w_qkv[:, :d_q], w_qkv[:, d_q:d_q + d_kv], w_qkv[:, d_q + d_kv:]
    b_q, b_k, b_v = b_qkv[:d_q], b_qkv[d_q:d_q + d_kv], b_qkv[d_q + d_kv:]
    b_vt = jnp.broadcast_to(b_v.astype(F32)[:, None], (d_kv, CHUNK))
    const = lambda shape: pl.BlockSpec(shape, lambda b, t, s: (0,) * len(shape))
    grid_spec = pltpu.PrefetchScalarGridSpec(
        num_scalar_prefetch=1,
        grid=(bsz, seq // tile),
        in_specs=[
            pl.BlockSpec((1, tile, d_model), lambda b, t, s: (b, t, 0)),
            const((1, d_model)),
            const((d_model, d_q)), const((1, d_q)),
            const((d_model, d_kv)), const((1, d_kv)),
            const((d_kv, d_model)), const((d_kv, CHUNK)),
            const((d_q, d_model)), const((1, d_model)),
        ],
        out_specs=pl.BlockSpec((1, tile, d_model), lambda b, t, s: (b, t, 0)),
        scratch_shapes=[
            pltpu.VMEM((n_blocks, ATTN_PAIRS * CHUNK, LANES), BF16),
            pltpu.VMEM((4, tile + CHUNK, LANES), BF16),
            pltpu.VMEM((ATTN_KV_HEADS, n_blocks + 1, HEADDIM, CHUNK), BF16),
            pltpu.VMEM((tile, d_q), BF16),
        ],
    )
    return pl.pallas_call(
        functools.partial(_attn_kernel, tile=tile),
        grid_spec=grid_spec,
        out_shape=jax.ShapeDtypeStruct(h.shape, F32),
        compiler_params=pltpu.CompilerParams(
            dimension_semantics=("arbitrary", "arbitrary"),
            vmem_limit_bytes=VMEM_LIMIT),
        name="attn",
    )(sinks.astype(F32), h, row(g), w_q.astype(BF16), row(b_q), w_k.astype(BF16), row(b_k),
      w_v.T.astype(BF16), b_vt, w_o.astype(BF16), row(b_o))


def kernel(x, norm_mix_g, norm_mlp_g, final_norm_g, w_in_even, w_out_even, gm_ln_g, gm_ln_b,
           gm_w_s, gm_b_s, ssm_conv_w, ssm_conv_b, ssm_dt_bias, ssm_a_log, ssm_d, ssm_norm_g,
           w_qkv, b_qkv, w_o, b_o, attn_sinks, w_up, w_down):
    depth = norm_mix_g.shape[0]
    w_up_bf = w_up.astype(BF16)
    w_down_bf = w_down.astype(BF16)
    h = x
    for i in range(depth):
        j = i // 2
        if i % 2 == 0:
            h = _mixer0(h, norm_mix_g[i], w_in_even[j], w_out_even[j], gm_ln_g[j], gm_ln_b[j],
                        gm_w_s[j], gm_b_s[j], ssm_conv_w[j], ssm_conv_b[j], ssm_dt_bias[j],
                        ssm_a_log[j], ssm_d[j], ssm_norm_g[j])
        else:
            h = _attn(h, norm_mix_g[i], w_qkv[j], b_qkv[j], w_o[j], b_o[j], attn_sinks[j])
        h = _mlp(h, norm_mlp_g[i], w_up_bf, w_down_bf, final_norm_g, layer=i,
                 final_norm=(i == depth - 1))
    return h
```

```python
import functools

import jax
import jax.numpy as jnp
from jax import lax
from jax.experimental import pallas as pl
from jax.experimental.pallas import tpu as pltpu

F32 = jnp.float32
BF16 = jnp.bfloat16

RMS_EPS = 1e-5
LN_EPS = 1e-5

LANES = 128
SUBLANES = 8
CHUNK = 128
HEADDIM = 64
HALF = LANES // 2
LOG2E = 1.4426950408889634

GM_GROUPS = 8
SSM_HEADS = 16
SSM_GROUPS = 4
SSM_STATE = 128
SSM_CONV = 4
SLAB = 512
CONV_CARRY = SUBLANES

ATTN_HEADS = 16
ATTN_KV_HEADS = 2
ATTN_PAIRS = ATTN_HEADS // 2
PAIRS_PER_KV = ATTN_PAIRS // ATTN_KV_HEADS
ATTN_PARTS = 2

VMEM_LIMIT = 56 * 1024 * 1024


def _dot(a, b):
    return jnp.dot(a, b, preferred_element_type=F32)


def _dot_nt(a, b):
    return lax.dot_general(a, b, (((1,), (1,)), ((), ())), preferred_element_type=F32)


def _split3(x):
    hi = x.astype(BF16)
    r1 = x - hi.astype(F32)
    mid = r1.astype(BF16)
    lo = (r1 - mid.astype(F32)).astype(BF16)
    return hi, mid, lo


def _dot_sel_lhs(sel, b):
    sel = sel.astype(BF16)
    hi, mid, lo = _split3(b)
    return _dot(sel, hi) + _dot(sel, mid) + _dot(sel, lo)


def _dot_sel_rhs(a, sel):
    sel = sel.astype(BF16)
    hi, mid, lo = _split3(a)
    return _dot(hi, sel) + _dot(mid, sel) + _dot(lo, sel)


def _rmsnorm(x, g):
    ms = jnp.mean(x * x, axis=-1, keepdims=True)
    return x * lax.rsqrt(ms + RMS_EPS) * g


def _gelu_tanh(x):
    c = 0.7978845608028654
    half = 0.5 * x
    return half + half * jnp.tanh(x * (c + (0.044715 * c) * (x * x)))


def _silu(x):
    half = 0.5 * x
    return half + half * jnp.tanh(half)


def _softplus(x):
    return jnp.maximum(x, 0.0) + jnp.log1p(jnp.exp(-jnp.abs(x)))


def _mixer0_kernel(x_ref, g_ref, w_in_ref, w_out_ref, lng_ref, lnb_ref, ws_ref, bexp_ref,
                   convw_ref, convb_ref, dtb_ref, alog_ref, dexp_ref, ng_ref, hexp_ref,
                   out_ref, carry_s, state_s, mixa_s, prev_x_s, prev_y_s, prev_z_s,
                   *, tile, tiles_per_seq, d_model, d_inner):
    n_chunks = tile // CHUNK
    d_bc = SSM_GROUPS * SSM_STATE
    step = pl.program_id(0)

    @pl.when(step == 0)
    def _():
        mixa_s[...] = jnp.zeros_like(mixa_s)
        prev_x_s[...] = jnp.zeros_like(prev_x_s)
        prev_y_s[...] = jnp.zeros_like(prev_y_s)
        prev_z_s[...] = jnp.zeros_like(prev_z_s)

    @pl.when(step % tiles_per_seq == 0)
    def _():
        state_s[...] = jnp.zeros_like(state_s)
        carry_s[...] = jnp.zeros_like(carry_s)

    row = lax.broadcasted_iota(jnp.int32, (CHUNK, CHUNK), 0)
    col = lax.broadcasted_iota(jnp.int32, (CHUNK, CHUNK), 1)
    causal = row >= col
    o_u, o_v, o_z, o_x = 0, d_model, 2 * d_model, 2 * d_model + d_inner
    o_dt = o_x + d_inner + 2 * d_bc
    conv_dim = d_inner + 2 * d_bc
    n_slabs_d = d_model // SLAB
    gw = d_inner // SSM_GROUPS


    mixa_prev = mixa_s[...]
    out_acc = [_dot(mixa_prev, w_out_ref[0:d_model, n * SLAB:(n + 1) * SLAB])
               for n in range(n_slabs_d)]
    x = x_ref[0]
    y = _rmsnorm(x, g_ref[...]).astype(BF16)

    def in_proj(lo, width=SLAB):
        return _dot(y, w_in_ref[:, lo:lo + width])

    def conv_silu(raw, lo):
        cols = slice(lo, lo + SLAB)
        head16 = jnp.concatenate([carry_s[:, cols], raw[0:CONV_CARRY, :]], axis=0)
        w_last = convw_ref[SSM_CONV - 1:SSM_CONV, cols]
        conv = convb_ref[:, cols] + w_last * raw
        conv8 = convb_ref[:, cols] + w_last * raw[0:CONV_CARRY, :]
        for k in range(1, SSM_CONV):
            w_k = convw_ref[SSM_CONV - 1 - k:SSM_CONV - k, cols]
            conv = conv + w_k * pltpu.roll(raw, k, axis=0)
            conv8 = conv8 + w_k * pltpu.roll(head16, k, axis=0)[CONV_CARRY:2 * CONV_CARRY, :]
        carry_s[:, cols] = raw[tile - CONV_CARRY:tile, :]
        return _silu(jnp.concatenate([conv8, conv[CONV_CARRY:, :]], axis=0))

    xbc_raw = [in_proj(o_x)]
    yv = prev_y_s[...] * _silu(prev_z_s[...])
    xbc_raw.append(in_proj(o_x + SLAB))
    b_out = jnp.concatenate(
        [_rmsnorm(yv[:, g * gw:(g + 1) * gw], ng_ref[:, g * gw:(g + 1) * gw]).astype(BF16)
         for g in range(SSM_GROUPS)], axis=1)
    xbc_act = []
    for n in range(2, conv_dim // SLAB):
        xbc_raw.append(in_proj(o_x + n * SLAB))
        xbc_act.append(conv_silu(xbc_raw[n - 2], (n - 2) * SLAB))
    dt_raw = in_proj(o_dt, LANES)
    u_raw = [in_proj(o_u)]
    xbc_act.append(conv_silu(xbc_raw[-2], conv_dim - 2 * SLAB))
    u_raw.append(in_proj(o_u + SLAB))
    xbc_act.append(conv_silu(xbc_raw[-1], conv_dim - SLAB))
    xbc = jnp.concatenate(xbc_act, axis=1)

    v_raw = [in_proj(o_v)]
    u = [_gelu_tanh(u_raw[0])]
    dt = _softplus(dt_raw + dtb_ref[...])
    a = dt * (-jnp.exp(alog_ref[...]))
    v_raw.append(in_proj(o_v + SLAB))
    u.append(_gelu_tanh(u_raw[1]))
    u = jnp.concatenate(u, axis=1)

    tril = jnp.where(causal, 1.0, 0.0).astype(F32)
    acums = [_dot_sel_lhs(tril, a[c * CHUNK:(c + 1) * CHUNK, :]) for c in range(n_chunks)]

    v = []
    for n in range(n_slabs_d):
        out_acc[n] = out_acc[n] + _dot(b_out, w_out_ref[d_model:, n * SLAB:(n + 1) * SLAB])
        v.append(_gelu_tanh(v_raw[n]))
    out_ref[0] = prev_x_s[...] + jnp.concatenate(out_acc, axis=1)
    v = jnp.concatenate(v, axis=1)

    chunk_terms = []
    for c in range(n_chunks):
        acum = acums[c]
        dtc = dt[c * CHUNK:(c + 1) * CHUNK, :]
        last = acum[CHUNK - 1:CHUNK, :]
        te_t = (jnp.exp(last - acum) * dtc).T
        decay = _dot_sel_rhs(jnp.exp(acum[CHUNK - SUBLANES:CHUNK, :]), hexp_ref[...])
        chunk_terms.append((acum, acum.T, dtc.T, te_t, decay[SUBLANES - 1:SUBLANES, :]))

    z_raw = [in_proj(o_z)]
    mu = jnp.mean(v, axis=-1, keepdims=True)
    vc = v - mu
    var = jnp.mean(vc * vc, axis=-1, keepdims=True)
    z_raw.append(in_proj(o_z + SLAB))
    vn = (vc * lax.rsqrt(var + LN_EPS) * lng_ref[...] + lnb_ref[...]).astype(BF16)
    prev_z_s[...] = jnp.concatenate(z_raw, axis=1)
    prev_x_s[...] = x

    xs = xbc[:, 0:d_inner]
    lo_full = lax.broadcasted_iota(jnp.int32, (1, d_inner), 1) % LANES < HALF
    x_lo = jnp.where(lo_full, xs, 0.0).astype(BF16)
    x_hi = jnp.where(lo_full, 0.0, xs).astype(BF16)
    bm = xbc[:, d_inner:d_inner + d_bc]
    cm = xbc[:, d_inner + d_bc:d_inner + 2 * d_bc]
    bm_bf = bm.astype(BF16)
    cm_bf = cm.astype(BF16)
    pairs_per_group = SSM_HEADS // SSM_GROUPS // 2
    state = state_s[...]
    y_chunks = []
    for c in range(n_chunks):
        cs = slice(c * CHUNK, (c + 1) * CHUNK)
        acum, acum_t, dt_t, te_t, chunk_decay = chunk_terms[c]

        st_lo = jnp.where(lo_full, state, 0.0).astype(BF16)
        st_hi = jnp.where(lo_full, 0.0, state).astype(BF16)

        y_pairs = []
        s_pairs = []
        for j in range(SSM_HEADS // 2):
            ps = slice(j * LANES, (j + 1) * LANES)
            if j % pairs_per_group == 0:
                g = j // pairs_per_group
                ns = slice(g * SSM_STATE, (g + 1) * SSM_STATE)
                bg = bm[cs, ns]
                cg = cm[cs, ns]
                cb = _dot_nt(cm_bf[cs, ns], bm_bf[cs, ns])
                bg_t = bg.T
            y_acc = None
            s_acc = None
            for h, x_sel, st_sel in ((2 * j, x_lo, st_lo), (2 * j + 1, x_hi, st_hi)):
                a_col = jnp.broadcast_to(acum[:, h:h + 1], (CHUNK, CHUNK))
                seg = a_col - acum_t[h:h + 1, :]
                decay = jnp.exp(jnp.where(causal, seg, -jnp.inf))
                w_intra = (cb * decay * dt_t[h:h + 1, :]).astype(BF16)
                c_in = (cg * jnp.exp(a_col)).astype(BF16)
                xh = x_sel[cs, ps]
                lhs = jnp.concatenate([w_intra, c_in], axis=1)
                rhs = jnp.concatenate([xh, st_sel[:, ps]], axis=0)
                yh = _dot(lhs, rhs)
                sh = _dot((bg_t * te_t[h:h + 1, :]).astype(BF16), xh)
                y_acc = yh if y_acc is None else y_acc + yh
                s_acc = sh if s_acc is None else s_acc + sh
            y_pairs.append(y_acc)
            s_pairs.append(s_acc)
        y_chunks.append(jnp.concatenate(y_pairs, axis=1))
        state = state * chunk_decay + jnp.concatenate(s_pairs, axis=1)
    state_s[...] = state

    prev_y_s[...] = jnp.concatenate(y_chunks, axis=0) + xs * dexp_ref[...]

    for g in range(GM_GROUPS):
        gs = slice(g * CHUNK, (g + 1) * CHUNK)
        w_g = jnp.where(causal, ws_ref[g], 0.0).astype(BF16)
        for c in range(n_chunks):
            cs = slice(c * CHUNK, (c + 1) * CHUNK)
            mixed = _dot(w_g, vn[cs, gs]) + bexp_ref[:, gs]
            mixa_s[cs, gs] = (u[cs, gs] * mixed).astype(BF16)


def _mixer0(h, g, w_in, w_out, ln_g, ln_b, w_s, b_s, conv_w, conv_b, dt_bias, a_log, d_skip,
            norm_g, *, tile=512):
    bsz, seq, d_model = h.shape
    d_inner = d_model
    d_bc = SSM_GROUPS * SSM_STATE
    conv_dim = d_inner + 2 * d_bc
    n_main = 3 * d_model + conv_dim
    assert w_in.shape == (d_model, n_main + SSM_HEADS)
    assert seq % tile == 0 and tile % CHUNK == 0 and d_inner == SSM_HEADS * HEADDIM

    pad = LANES - SSM_HEADS
    w_in_p = jnp.concatenate([w_in.astype(BF16), jnp.zeros((d_model, pad), BF16)], axis=1)
    row = lambda v: v.reshape(1, -1).astype(F32)
    pad_row = lambda v: jnp.pad(v.astype(F32), (0, pad)).reshape(1, LANES)
    bexp = jnp.repeat(b_s.T.astype(F32), CHUNK, axis=1)
    dexp = jnp.repeat(d_skip.astype(F32), HEADDIM).reshape(1, d_inner)
    hexp = (jnp.arange(LANES)[:, None] == (jnp.arange(d_inner)[None, :] // HEADDIM)).astype(F32)

    tiles_per_seq = seq // tile
    n_tiles = bsz * tiles_per_seq

    def tile_index(i):
        return i // tiles_per_seq, i % tiles_per_seq, 0

    const = lambda shape: pl.BlockSpec(shape, lambda i: (0,) * len(shape))
    kern = functools.partial(_mixer0_kernel, tile=tile, tiles_per_seq=tiles_per_seq,
                             d_model=d_model, d_inner=d_inner)
    return pl.pallas_call(
        kern,
        grid=(n_tiles + 1,),
        in_specs=[
            pl.BlockSpec((1, tile, d_model), lambda i: tile_index(jnp.minimum(i, n_tiles - 1))),
            const((1, d_model)),
            pl.BlockSpec(w_in_p.shape, lambda i: (0, 0), pipeline_mode=pl.Buffered(1)),
            pl.BlockSpec(w_out.shape, lambda i: (0, 0), pipeline_mode=pl.Buffered(1)),
            const((1, d_model)), const((1, d_model)),
            const(w_s.shape),
            const(bexp.shape),
            const(conv_w.shape), const((1, conv_dim)),
            const((1, LANES)), const((1, LANES)),
            const((1, d_inner)), const((1, d_inner)),
            const(hexp.shape),
        ],
        out_specs=pl.BlockSpec((1, tile, d_model), lambda i: tile_index(jnp.maximum(i - 1, 0))),
        out_shape=jax.ShapeDtypeStruct(h.shape, F32),
        scratch_shapes=[
            pltpu.VMEM((CONV_CARRY, conv_dim), F32),
            pltpu.VMEM((SSM_STATE, d_inner), F32),
            pltpu.VMEM((tile, d_model), BF16),
            pltpu.VMEM((tile, d_model), F32),
            pltpu.VMEM((tile, d_inner), F32),
            pltpu.VMEM((tile, d_inner), F32),
        ],
        compiler_params=pltpu.CompilerParams(
            dimension_semantics=("arbitrary",),
            vmem_limit_bytes=VMEM_LIMIT),
        name="mixer0",
    )(h, row(g), w_in_p, w_out.astype(BF16), row(ln_g), row(ln_b), w_s.astype(F32), bexp,
      conv_w.astype(F32), row(conv_b), pad_row(dt_bias), pad_row(a_log), dexp, row(norm_g),
      hexp)


def _mlp_kernel(x_ref, g_ref, w_up_ref, w_down_ref, gf_ref, out_ref, *, ff_chunk, final_norm):
    x = x_ref[...]
    y = _rmsnorm(x, g_ref[...]).astype(BF16)
    acc = x
    d_ff = w_up_ref.shape[1]
    for c in range(d_ff // ff_chunk):
        cs = slice(c * ff_chunk, (c + 1) * ff_chunk)
        hid = jnp.maximum(_dot(y, w_up_ref[:, cs]), 0.0)
        acc = acc + _dot((hid * hid).astype(BF16), w_down_ref[cs, :])
    if final_norm:
        acc = _rmsnorm(acc, gf_ref[...])
    out_ref[...] = acc


def _mlp(h, g, w_up, w_down, g_final, *, layer, final_norm, tile=1024, ff_chunk=1024):
    bsz, seq, d_model = h.shape
    n = bsz * seq
    d_ff = w_up.shape[2]
    assert n % tile == 0 and d_ff % ff_chunk == 0
    x2 = h.reshape(n, d_model)
    const = lambda shape: pl.BlockSpec(shape, lambda i: (0,) * len(shape))
    of_layer = lambda shape: pl.BlockSpec((None,) + shape, lambda i: (layer, 0, 0),
                                          pipeline_mode=pl.Buffered(1))
    out = pl.pallas_call(
        functools.partial(_mlp_kernel, ff_chunk=ff_chunk, final_norm=final_norm),
        grid=(n // tile,),
        in_specs=[
            pl.BlockSpec((tile, d_model), lambda i: (i, 0)),
            const((1, d_model)),
            of_layer((d_model, d_ff)),
            of_layer((d_ff, d_model)),
            const((1, d_model)),
        ],
        out_specs=pl.BlockSpec((tile, d_model), lambda i: (i, 0)),
        out_shape=jax.ShapeDtypeStruct((n, d_model), F32),
        compiler_params=pltpu.CompilerParams(
            dimension_semantics=("arbitrary",), vmem_limit_bytes=VMEM_LIMIT),
        name="mlp_final" if final_norm else "mlp",
    )(x2, g.reshape(1, -1).astype(F32), w_up, w_down, g_final.reshape(1, -1).astype(F32))
    return out.reshape(bsz, seq, d_model)


def _attn_kernel(sinks_ref, x_ref, g_ref, wq_ref, bq_ref, wk_ref, bk_ref, wvt_ref, bvt_ref,
                 wo_ref, bo_ref, out_ref, q_s, k_s, vt_s, att_s, *, tile):
    n_blocks = tile // CHUNK
    grp_rows = PAIRS_PER_KV * CHUNK
    t_idx = pl.program_id(1)

    @pl.when(t_idx == 0)
    def _():
        k_s[:, 0:CHUNK, :] = jnp.zeros((4, CHUNK, LANES), BF16)
        vt_s[:, 0] = jnp.zeros((ATTN_KV_HEADS, HEADDIM, CHUNK), BF16)

    @pl.when(t_idx != 0)
    def _():
        k_s[:, 0:CHUNK, :] = k_s[:, tile:tile + CHUNK, :]
        vt_s[:, 0] = vt_s[:, n_blocks]

    n_var = 2 * ATTN_KV_HEADS
    part_rows = tile // ATTN_PARTS
    blocks_per_part = n_blocks // ATTN_PARTS
    q_slab = 2 * LANES
    lo_lane = lax.broadcasted_iota(jnp.int32, (1, LANES), 1) < HALF
    bvt = jnp.concatenate([bvt_ref[...]] * blocks_per_part, axis=1)

    def rows_of(part):
        return slice(part * part_rows, (part + 1) * part_rows)

    y_parts = [_rmsnorm(x_ref[0, rows_of(part), :], g_ref[...]).astype(BF16)
               for part in range(ATTN_PARTS)]

    def q_piece(part, n):
        cols = slice(n * q_slab, (n + 1) * q_slab)
        q = ((_dot(y_parts[part], wq_ref[:, cols]) + bq_ref[:, cols])
             * (HEADDIM ** -0.5 * LOG2E)).astype(BF16)
        for b in range(blocks_per_part):
            for jj in range(q_slab // LANES):
                j = n * (q_slab // LANES) + jj
                q_s[part * blocks_per_part + b, j * CHUNK:(j + 1) * CHUNK, :] = (
                    q[b * CHUNK:(b + 1) * CHUNK, jj * LANES:(jj + 1) * LANES])

    def k_piece(part):
        k = _dot(y_parts[part], wk_ref[...]) + bk_ref[...]
        k_swp = pltpu.roll(k, HALF, axis=1)
        k_variants = (jnp.where(lo_lane, k, 0.0), jnp.where(lo_lane, 0.0, k_swp),
                      jnp.where(lo_lane, k_swp, 0.0), jnp.where(lo_lane, 0.0, k))
        lo = CHUNK + part * part_rows
        for g, val in enumerate(k_variants):
            k_s[g, lo:lo + part_rows, :] = val.astype(BF16)

    def v_piece(part):
        vt = _dot_nt(wvt_ref[...], y_parts[part]) + bvt
        for kv in range(ATTN_KV_HEADS):
            for b in range(blocks_per_part):
                vt_s[kv, 1 + part * blocks_per_part + b] = (
                    vt[kv * HEADDIM:(kv + 1) * HEADDIM, b * CHUNK:(b + 1) * CHUNK].astype(BF16))

    def o_piece(part, n):
        cols = slice(n * q_slab, (n + 1) * q_slab)
        rows = rows_of(part)
        out_ref[0, rows, cols] = (x_ref[0, rows, cols] + _dot(att_s[rows, :], wo_ref[:, cols])
                                  + bo_ref[:, cols])

    n_q_pieces = wq_ref.shape[1] // q_slab
    n_o_pieces = wo_ref.shape[1] // q_slab
    for n in range(n_q_pieces):
        q_piece(0, n)
    k_piece(0)
    v_piece(0)
    fillers = []
    for part in range(ATTN_PARTS):
        todo = []
        if part + 1 < ATTN_PARTS:
            todo += [functools.partial(q_piece, part + 1, n) for n in range(n_q_pieces)]
            todo += [functools.partial(k_piece, part + 1), functools.partial(v_piece, part + 1)]
        if part > 0:
            todo += [functools.partial(o_piece, part - 1, n) for n in range(n_o_pieces)]
        fillers.append(todo)
    slots_per_part = blocks_per_part * n_var

    si = lax.broadcasted_iota(jnp.int32, (2 * CHUNK, CHUNK), 0)
    qi = lax.broadcasted_iota(jnp.int32, (2 * CHUNK, CHUNK), 1)
    valid = (si > qi) & (si <= qi + CHUNK)
    neg_inf = jnp.float32(-jnp.inf)
    bias = jnp.where(valid, 0.0, neg_inf)
    bias_first = jnp.where(valid & (si >= CHUNK), 0.0, neg_inf)

    def scores_t(i, g):
        kv = g // 2
        return _dot_nt(k_s[g, i * CHUNK:(i + 2) * CHUNK, :],
                       q_s[i, kv * grp_rows:(kv + 1) * grp_rows, :])

    st_next = scores_t(0, 0)
    for i in range(n_blocks):
        blk_bias = jnp.where(t_idx == 0, bias_first, bias) if i == 0 else bias
        todo = fillers[i // blocks_per_part]
        if i % blocks_per_part == 0:
            slots_left = slots_per_part
        for kv in range(ATTN_KV_HEADS):
            p_cols = []
            inv_cols = []
            for par in range(2):
                g = 2 * kv + par
                st_all = st_next
                if g + 1 < n_var:
                    st_next = scores_t(i, g + 1)
                elif i + 1 < n_blocks:
                    st_next = scores_t(i + 1, 0)
                for jj in range(PAIRS_PER_KV):
                    head = 2 * (kv * PAIRS_PER_KV + jj) + par
                    s = st_all[:, jj * CHUNK:(jj + 1) * CHUNK] + blk_bias
                    sink = sinks_ref[head] * LOG2E
                    m = jnp.maximum(jnp.max(s, axis=0, keepdims=True), sink)
                    p = jnp.exp2(s - m)
                    denom = jnp.sum(p, axis=0, keepdims=True) + jnp.exp2(sink - m)
                    p_cols.append(p.astype(BF16))
                    inv_cols.append(1.0 / denom)
                if par == 1:
                    v_band = jnp.concatenate([vt_s[kv, i], vt_s[kv, i + 1]], axis=1)
                    ot = _dot(v_band, jnp.concatenate(p_cols, axis=1))
                    ot = ot * jnp.concatenate(inv_cols, axis=1)
                for _ in range(-(-len(todo) // slots_left)):
                    todo.pop(0)()
                slots_left -= 1
            for jj in range(PAIRS_PER_KV):
                j = kv * PAIRS_PER_KV + jj
                o_pair = jnp.concatenate(
                    [ot[:, jj * CHUNK:(jj + 1) * CHUNK],
                     ot[:, (PAIRS_PER_KV + jj) * CHUNK:(PAIRS_PER_KV + jj + 1) * CHUNK]], axis=0)
                att_s[i * CHUNK:(i + 1) * CHUNK, j * LANES:(j + 1) * LANES] = (
                    o_pair.T.astype(BF16))

    assert not any(fillers)
    for n in range(n_o_pieces):
        o_piece(ATTN_PARTS - 1, n)


def _attn(h, g, w_qkv, b_qkv, w_o, b_o, sinks, *, tile=1024):
    bsz, seq, d_model = h.shape
    d_q = ATTN_HEADS * HEADDIM
    d_kv = ATTN_KV_HEADS * HEADDIM
    assert w_qkv.shape == (d_model, d_q + 2 * d_kv) and d_kv == LANES
    assert seq % tile == 0 and tile % CHUNK == 0
    n_blocks = tile // CHUNK
    row = lambda v: v.reshape(1, -1).astype(F32)
    w_q, w_k, w_v = w_qkv[:, :d_q], w_qkv[:, d_q:d_q + d_kv], w_qkv[:, d_q + d_kv:]
    b_q, b_k, b_v = b_qkv[:d_q], b_qkv[d_q:d_q + d_kv], b_qkv[d_q + d_kv:]
    b_vt = jnp.broadcast_to(b_v.astype(F32)[:, None], (d_kv, CHUNK))
    const = lambda shape: pl.BlockSpec(shape, lambda b, t, s: (0,) * len(shape))
    grid_spec = pltpu.PrefetchScalarGridSpec(
        num_scalar_prefetch=1,
        grid=(bsz, seq // tile),
        in_specs=[
            pl.BlockSpec((1, tile, d_model), lambda b, t, s: (b, t, 0)),
            const((1, d_model)),
            const((d_model, d_q)), const((1, d_q)),
            const((d_model, d_kv)), const((1, d_kv)),
            const((d_kv, d_model)), const((d_kv, CHUNK)),
            const((d_q, d_model)), const((1, d_model)),
        ],
        out_specs=pl.BlockSpec((1, tile, d_model), lambda b, t, s: (b, t, 0)),
        scratch_shapes=[
            pltpu.VMEM((n_blocks, ATTN_PAIRS * CHUNK, LANES), BF16),
            pltpu.VMEM((4, tile + CHUNK, LANES), BF16),
            pltpu.VMEM((ATTN_KV_HEADS, n_blocks + 1, HEADDIM, CHUNK), BF16),
            pltpu.VMEM((tile, d_q), BF16),
        ],
    )
    return pl.pallas_call(
        functools.partial(_attn_kernel, tile=tile),
        grid_spec=grid_spec,
        out_shape=jax.ShapeDtypeStruct(h.shape, F32),
        compiler_params=pltpu.CompilerParams(
            dimension_semantics=("arbitrary", "arbitrary"),
            vmem_limit_bytes=VMEM_LIMIT),
        name="attn",
    )(sinks.astype(F32), h, row(g), w_q.astype(BF16), row(b_q), w_k.astype(BF16), row(b_k),
      w_v.T.astype(BF16), b_vt, w_o.astype(BF16), row(b_o))


def kernel(x, norm_mix_g, norm_mlp_g, final_norm_g, w_in_even, w_out_even, gm_ln_g, gm_ln_b,
           gm_w_s, gm_b_s, ssm_conv_w, ssm_conv_b, ssm_dt_bias, ssm_a_log, ssm_d, ssm_norm_g,
           w_qkv, b_qkv, w_o, b_o, attn_sinks, w_up, w_down):
    depth = norm_mix_g.shape[0]
    w_up_bf = w_up.astype(BF16)
    w_down_bf = w_down.astype(BF16)
    h = x
    for i in range(depth):
        j = i // 2
        if i % 2 == 0:
            h = _mixer0(h, norm_mix_g[i], w_in_even[j], w_out_even[j], gm_ln_g[j], gm_ln_b[j],
                        gm_w_s[j], gm_b_s[j], ssm_conv_w[j], ssm_conv_b[j], ssm_dt_bias[j],
                        ssm_a_log[j], ssm_d[j], ssm_norm_g[j])
        else:
            h = _attn(h, norm_mix_g[i], w_qkv[j], b_qkv[j], w_o[j], b_o[j], attn_sinks[j])
        h = _mlp(h, norm_mlp_g[i], w_up_bf, w_down_bf, final_norm_g, layer=i,
                 final_norm=(i == depth - 1))
    return h
```
